```python
import math
import jax, jax.numpy as jnp
from jax import lax
import numpy as np

D_MODEL = 4096
BATCH = 4
SEQ = 2048
DEPTH = 2
DEC_BATCH = 8
DEC_SEQ = 1
PAST_LEN = 16384
PAGE_SIZE = 128

N_META = 16
D_FF = ((8 * D_MODEL // 3 + 255) // 256) * 256
D_CONV = D_MODEL // 4
CONV_WIDTH = 31
N_HEADS = 8
HEAD_DIM = D_MODEL // 32
D_ATTN = N_HEADS * 2 * HEAD_DIM
D_SSM = D_MODEL // 4
SSM_GROUP = 16
N_GROUPS = D_SSM // SSM_GROUP
STATE_DIM = 64
DT_MIN = 1e-3
DT_MAX = 1e-1
Q_BLOCK = 128
EPS = 1e-6
ATTN_SCALE = 1.0 / math.sqrt(HEAD_DIM)
NEG_BIG = -1e30
SPLITS = (2 * D_CONV, 2 * D_CONV + D_ATTN, 2 * D_CONV + 2 * D_ATTN, 2 * D_CONV + 3 * D_ATTN,
          2 * D_CONV + 3 * D_ATTN + D_SSM)
IN_COLS = 2 * D_CONV + 3 * D_ATTN + D_SSM + 3 * D_MODEL

kernel_name = "hybrid_conv_diffattn_s5_decoder_step"


def rms_norm(x, g):
    xf = x.astype(jnp.float32)
    y = xf * lax.rsqrt(jnp.mean(xf * xf, axis=-1, keepdims=True) + EPS)
    return (y * g.astype(jnp.float32)).astype(x.dtype)


def layer_norm(x, g, b):
    xf = x.astype(jnp.float32)
    mu = jnp.mean(xf, axis=-1, keepdims=True)
    var = jnp.mean(jnp.square(xf - mu), axis=-1, keepdims=True)
    y = (xf - mu) * lax.rsqrt(var + EPS)
    return (y * g.astype(jnp.float32) + b.astype(jnp.float32)).astype(x.dtype)


def swiglu(h, w_gate, w_up, w_down):
    return (jax.nn.silu(h @ w_gate) * (h @ w_up)) @ w_down


def conv_branch(z_conv, conv_state, w, b, ln_g, ln_b, proj):
    a, gate = jnp.split(z_conv, 2, axis=-1)
    u = a * jax.nn.sigmoid(gate)
    ext = jnp.concatenate([conv_state.astype(u.dtype), u], axis=1)
    y = lax.conv_general_dilated(ext, w.astype(u.dtype)[:, None, :], window_strides=(1,), padding="VALID",
                                 dimension_numbers=("NWC", "WIO", "NWC"), feature_group_count=D_CONV)
    y = jax.nn.silu(layer_norm(y + b, ln_g, ln_b))
    return y @ proj, ext[:, -(CONV_WIDTH - 1):]


def diff_attention(q, k, v, q_start, lam):
    bn, tq = q.shape[0], q.shape[1]
    tk = k.shape[1]
    blk = min(Q_BLOCK, tq)
    nb = -(-tq // blk)
    pad = nb * blk - tq
    qb = jnp.pad(q, ((0, 0), (0, pad), (0, 0), (0, 0)))
    qb = qb.reshape(bn, nb, blk, N_HEADS, 2, HEAD_DIM).transpose(1, 0, 2, 3, 4, 5)
    qpos = (q_start + jnp.arange(nb * blk)).reshape(nb, blk)
    k2 = k.reshape(bn, tk, N_HEADS, 2, HEAD_DIM)
    kpos = jnp.arange(tk)

    def one_block(args):
        qblk, pos = args
        s = jnp.einsum("bqhcd,bkhcd->bchqk", qblk, k2, preferred_element_type=jnp.float32) * ATTN_SCALE
        mask = kpos[None, :] <= pos[:, None]
        p = jax.nn.softmax(jnp.where(mask, s, NEG_BIG), axis=-1)
        wts = p[:, 0] - lam * p[:, 1]
        return jnp.einsum("bhqk,bkhe->bqhe", wts.astype(v.dtype), v)

    out = lax.map(one_block, (qb, qpos))
    return out.transpose(1, 0, 2, 3, 4).reshape(bn, nb * blk, N_HEADS, 2 * HEAD_DIM)[:, :tq]


def ssm_branch(z_ssm, h_re, h_im, a_re, a_im, log_dt, b_re, b_im, c_re, c_im, d, w_glu, proj):
    f32 = jnp.float32
    bn, t = z_ssm.shape[0], z_ssm.shape[1]
    uf = z_ssm.astype(f32).reshape(bn, t, N_GROUPS, SSM_GROUP)
    dt = jnp.exp(log_dt.astype(f32))[:, None]
    ar, ai = a_re.astype(f32), a_im.astype(f32)
    mag = jnp.exp(ar * dt)
    ang = ai * dt
    abar_re, abar_im = mag * jnp.cos(ang), mag * jnp.sin(ang)
    den = ar * ar + ai * ai
    nr, ni = abar_re - 1.0, abar_im
    f_re = (nr * ar + ni * ai) / den
    f_im = (ni * ar - nr * ai) / den
    br, bi = b_re.astype(f32), b_im.astype(f32)
    bb_re = f_re[..., None] * br - f_im[..., None] * bi
    bb_im = f_re[..., None] * bi + f_im[..., None] * br
    bu_re = jnp.einsum("gpj,btgj->btgp", bb_re, uf)
    bu_im = jnp.einsum("gpj,btgj->btgp", bb_im, uf)
    hr, hi = h_re.astype(f32), h_im.astype(f32)
    bu_re = bu_re.at[:, 0].add(abar_re * hr - abar_im * hi)
    bu_im = bu_im.at[:, 0].add(abar_re * hi + abar_im * hr)
    a_r = jnp.broadcast_to(abar_re, bu_re.shape)
    a_i = jnp.broadcast_to(abar_im, bu_im.shape)

    def combine(e1, e2):
        a1r, a1i, b1r, b1i = e1
        a2r, a2i, b2r, b2i = e2
        return (a1r * a2r - a1i * a2i, a1r * a2i + a1i * a2r,
                a2r * b1r - a2i * b1i + b2r, a2r * b1i + a2i * b1r + b2i)

    _, _, xr, xi = lax.associative_scan(combine, (a_r, a_i, bu_re, bu_im), axis=1)
    y = (jnp.einsum("gjp,btgp->btgj", c_re.astype(f32), xr)
         - jnp.einsum("gjp,btgp->btgj", c_im.astype(f32), xi)
         + d.astype(f32).reshape(N_GROUPS, SSM_GROUP) * uf)
    y = jax.nn.gelu(y.reshape(bn, t, D_SSM).astype(z_ssm.dtype))
    y = y * jax.nn.sigmoid(y @ w_glu)
    return y @ proj, xr[:, -1].astype(h_re.dtype), xi[:, -1].astype(h_im.dtype)


def trunk_layer(x, p, lam_init, conv_state, ssm_re, ssm_im, k_past, v_past):
    bn, t = x.shape[0], x.shape[1]
    x = x + 0.5 * swiglu(rms_norm(x, p["norm_ffn1"]), p["ffn1_gate"], p["ffn1_up"], p["ffn1_down"])
    h = rms_norm(x, p["norm_mix"])
    z = h @ p["w_in"]
    z_conv, z_q, z_k, z_v, z_ssm, z_gate = jnp.split(z, SPLITS, axis=-1)

    conv_out, conv_new = conv_branch(z_conv, conv_state, p["conv_w"], p["conv_b"],
                                     p["conv_ln_g"], p["conv_ln_b"], p["conv_proj"])

    q = z_q.reshape(bn, t, N_HEADS, 2 * HEAD_DIM)
    k_new = z_k.reshape(bn, t, N_HEADS, 2 * HEAD_DIM)
    v_new = z_v.reshape(bn, t, N_HEADS, 2 * HEAD_DIM)
    if k_past is None:
        k_all, v_all, q_start = k_new, v_new, 0
    else:
        k_all = jnp.concatenate([k_past.astype(k_new.dtype), k_new], axis=1)
        v_all = jnp.concatenate([v_past.astype(v_new.dtype), v_new], axis=1)
        q_start = k_past.shape[1]
    f32 = jnp.float32
    lam = (jnp.exp(jnp.sum(p["lambda_q1"].astype(f32) * p["lambda_k1"].astype(f32)))
           - jnp.exp(jnp.sum(p["lambda_q2"].astype(f32) * p["lambda_k2"].astype(f32))) + lam_init)
    att = diff_attention(q, k_all, v_all, q_start, lam)
    att = rms_norm(att, p["attn_subln_g"]) * (1.0 - lam_init)
    attn_out = att.reshape(bn, t, D_ATTN) @ p["attn_proj"]

    ssm_out, s_re, s_im = ssm_branch(z_ssm, ssm_re, ssm_im, p["ssm_a_re"], p["ssm_a_im"], p["ssm_log_dt"],
                                     p["ssm_b_re"], p["ssm_b_im"], p["ssm_c_re"], p["ssm_c_im"],
                                     p["ssm_d"], p["ssm_glu"], p["ssm_proj"])

    g_a, g_b, g_c = jnp.split(jax.nn.sigmoid(z_gate), 3, axis=-1)
    merged = g_a * conv_out + g_b * attn_out + g_c * ssm_out
    x = x + merged @ p["w_out"]
    x = x + 0.5 * swiglu(rms_norm(x, p["norm_ffn2"]), p["ffn2_gate"], p["ffn2_up"], p["ffn2_down"])
    return x, k_new, v_new, conv_new, s_re, s_im


def setup_inputs(seed: int = 0) -> dict:
    key = jax.random.key(seed)
    ks = iter(jax.random.split(key, 64))
    f32 = jnp.float32

    def nrm(shape, scale):
        return jax.random.normal(next(ks), shape, f32) * scale

    def gain(shape):
        return 1.0 + nrm(shape, 0.02)

    n_pages = PAST_LEN // PAGE_SIZE
    n_phys = (DEC_BATCH * n_pages * 5) // 4
    kv_shape = (DEPTH, n_phys, PAGE_SIZE, N_HEADS, 2 * HEAD_DIM)
    x_prompt = nrm((BATCH, SEQ, D_MODEL), 1.0)
    x_sample = nrm((DEC_BATCH, DEC_SEQ, D_MODEL), 1.0)
    cache_k = nrm(kv_shape, 1.0)
    cache_v = nrm(kv_shape, 1.0)
    page_table = jax.random.permutation(next(ks), n_phys)[: DEC_BATCH * n_pages].reshape(
        DEC_BATCH, n_pages).astype(jnp.int32)
    state_conv = nrm((DEPTH, DEC_BATCH, CONV_WIDTH - 1, D_CONV), 0.5)
    state_ssm_re = nrm((DEPTH, DEC_BATCH, N_GROUPS, STATE_DIM), 0.5)
    state_ssm_im = nrm((DEPTH, DEC_BATCH, N_GROUPS, STATE_DIM), 0.5)
    a_im = jnp.broadcast_to(math.pi * jnp.arange(STATE_DIM, dtype=f32), (DEPTH, N_GROUPS, STATE_DIM))
    return {
        "x_prompt": x_prompt,
        "x_sample": x_sample,
        "cache_k": cache_k,
        "cache_v": cache_v,
        "page_table": page_table,
        "state_conv": state_conv,
        "state_ssm_re": state_ssm_re,
        "state_ssm_im": state_ssm_im,
        "meta_tokens": nrm((N_META, D_MODEL), 1.0),
        "norm_ffn1": gain((DEPTH, D_MODEL)),
        "ffn1_gate": nrm((DEPTH, D_MODEL, D_FF), D_MODEL ** -0.5),
        "ffn1_up": nrm((DEPTH, D_MODEL, D_FF), D_MODEL ** -0.5),
        "ffn1_down": nrm((DEPTH, D_FF, D_MODEL), D_FF ** -0.5),
        "norm_mix": gain((DEPTH, D_MODEL)),
        "w_in": nrm((DEPTH, D_MODEL, IN_COLS), D_MODEL ** -0.5),
        "conv_w": nrm((DEPTH, CONV_WIDTH, D_CONV), CONV_WIDTH ** -0.5),
        "conv_b": nrm((DEPTH, D_CONV), 0.01),
        "conv_ln_g": gain((DEPTH, D_CONV)),
        "conv_ln_b": nrm((DEPTH, D_CONV), 0.01),
        "conv_proj": nrm((DEPTH, D_CONV, D_MODEL), D_CONV ** -0.5),
        "lambda_q1": nrm((DEPTH, HEAD_DIM), 0.1),
        "lambda_k1": nrm((DEPTH, HEAD_DIM), 0.1),
        "lambda_q2": nrm((DEPTH, HEAD_DIM), 0.1),
        "lambda_k2": nrm((DEPTH, HEAD_DIM), 0.1),
        "attn_subln_g": gain((DEPTH, 2 * HEAD_DIM)),
        "attn_proj": nrm((DEPTH, D_ATTN, D_MODEL), D_ATTN ** -0.5),
        "ssm_a_re": -0.5 + nrm((DEPTH, N_GROUPS, STATE_DIM), 0.01),
        "ssm_a_im": a_im,
        "ssm_log_dt": jax.random.uniform(next(ks), (DEPTH, N_GROUPS), f32, math.log(DT_MIN), math.log(DT_MAX)),
        "ssm_b_re": nrm((DEPTH, N_GROUPS, STATE_DIM, SSM_GROUP), (2 * SSM_GROUP) ** -0.5),
        "ssm_b_im": nrm((DEPTH, N_GROUPS, STATE_DIM, SSM_GROUP), (2 * SSM_GROUP) ** -0.5),
        "ssm_c_re": nrm((DEPTH, N_GROUPS, SSM_GROUP, STATE_DIM), STATE_DIM ** -0.5),
        "ssm_c_im": nrm((DEPTH, N_GROUPS, SSM_GROUP, STATE_DIM), STATE_DIM ** -0.5),
        "ssm_d": nrm((DEPTH, D_SSM), 1.0),
        "ssm_glu": nrm((DEPTH, D_SSM, D_SSM), D_SSM ** -0.5),
        "ssm_proj": nrm((DEPTH, D_SSM, D_MODEL), D_SSM ** -0.5),
        "w_out": nrm((DEPTH, D_MODEL, D_MODEL), D_MODEL ** -0.5),
        "norm_ffn2": gain((DEPTH, D_MODEL)),
        "ffn2_gate": nrm((DEPTH, D_MODEL, D_FF), D_MODEL ** -0.5),
        "ffn2_up": nrm((DEPTH, D_MODEL, D_FF), D_MODEL ** -0.5),
        "ffn2_down": nrm((DEPTH, D_FF, D_MODEL), D_FF ** -0.5),
        "norm_final": gain((D_MODEL,)),
    }


def reference(x_prompt, x_sample, cache_k, cache_v, page_table, state_conv, state_ssm_re, state_ssm_im,
              meta_tokens, norm_ffn1, ffn1_gate, ffn1_up, ffn1_down, norm_mix, w_in,
              conv_w, conv_b, conv_ln_g, conv_ln_b, conv_proj,
              lambda_q1, lambda_k1, lambda_q2, lambda_k2, attn_subln_g, attn_proj,
              ssm_a_re, ssm_a_im, ssm_log_dt, ssm_b_re, ssm_b_im, ssm_c_re, ssm_c_im, ssm_d, ssm_glu, ssm_proj,
              w_out, norm_ffn2, ffn2_gate, ffn2_up, ffn2_down, norm_final):
    bp = x_prompt.shape[0]
    db = x_sample.shape[0]
    xp = jnp.concatenate([jnp.broadcast_to(meta_tokens.astype(x_prompt.dtype)[None], (bp, N_META, D_MODEL)),
                          x_prompt], axis=1)
    xs = x_sample
    zero_conv = jnp.zeros((bp, CONV_WIDTH - 1, D_CONV), x_prompt.dtype)
    zero_ssm = jnp.zeros((bp, N_GROUPS, STATE_DIM), x_prompt.dtype)
    kp_l, vp_l, cp_l, srp_l, sip_l = [], [], [], [], []
    ks_l, vs_l, cs_l, srs_l, sis_l = [], [], [], [], []
    for l in range(DEPTH):
        p = {
            "norm_ffn1": norm_ffn1[l], "ffn1_gate": ffn1_gate[l], "ffn1_up": ffn1_up[l], "ffn1_down": ffn1_down[l],
            "norm_mix": norm_mix[l], "w_in": w_in[l],
            "conv_w": conv_w[l], "conv_b": conv_b[l], "conv_ln_g": conv_ln_g[l], "conv_ln_b": conv_ln_b[l],
            "conv_proj": conv_proj[l],
            "lambda_q1": lambda_q1[l], "lambda_k1": lambda_k1[l], "lambda_q2": lambda_q2[l], "lambda_k2": lambda_k2[l],
            "attn_subln_g": attn_subln_g[l], "attn_proj": attn_proj[l],
            "ssm_a_re": ssm_a_re[l], "ssm_a_im": ssm_a_im[l], "ssm_log_dt": ssm_log_dt[l],
            "ssm_b_re": ssm_b_re[l], "ssm_b_im": ssm_b_im[l], "ssm_c_re": ssm_c_re[l], "ssm_c_im": ssm_c_im[l],
            "ssm_d": ssm_d[l], "ssm_glu": ssm_glu[l], "ssm_proj": ssm_proj[l],
            "w_out": w_out[l],
            "norm_ffn2": norm_ffn2[l], "ffn2_gate": ffn2_gate[l], "ffn2_up": ffn2_up[l], "ffn2_down": ffn2_down[l],
        }
        lam_init = 0.8 - 0.6 * math.exp(-0.3 * l)
        xp, kp, vp, cp, srp, sip = trunk_layer(xp, p, lam_init, zero_conv, zero_ssm, zero_ssm, None, None)
        k_past = cache_k[l, page_table].reshape(db, -1, N_HEADS, 2 * HEAD_DIM)
        v_past = cache_v[l, page_table].reshape(db, -1, N_HEADS, 2 * HEAD_DIM)
        xs, kn, vn, cn, srn, sin_ = trunk_layer(xs, p, lam_init, state_conv[l], state_ssm_re[l], state_ssm_im[l],
                                                k_past, v_past)
        kp_l.append(kp); vp_l.append(vp); cp_l.append(cp); srp_l.append(srp); sip_l.append(sip)
        ks_l.append(kn); vs_l.append(vn); cs_l.append(cn); srs_l.append(srn); sis_l.append(sin_)
    y_prompt = rms_norm(xp, norm_final)[:, N_META:]
    y_sample = rms_norm(xs, norm_final)
    new_k_prompt = jnp.stack(kp_l)
    new_v_prompt = jnp.stack(vp_l)
    new_conv_prompt = jnp.stack(cp_l)
    new_ssm_re_prompt = jnp.stack(srp_l)
    new_ssm_im_prompt = jnp.stack(sip_l)
    new_k_sample = jnp.stack(ks_l)
    new_v_sample = jnp.stack(vs_l)
    new_conv_sample = jnp.stack(cs_l)
    new_ssm_re_sample = jnp.stack(srs_l)
    new_ssm_im_sample = jnp.stack(sis_l)
    return (y_prompt, y_sample, new_k_prompt, new_v_prompt, new_conv_prompt, new_ssm_re_prompt, new_ssm_im_prompt,
            new_k_sample, new_v_sample, new_conv_sample, new_ssm_re_sample, new_ssm_im_sample)
```

```python
import functools
import math

import jax
import jax.numpy as jnp
from jax import lax
from jax.experimental import pallas as pl
from jax.experimental.pallas import tpu as pltpu

F32 = jnp.float32
BF16 = jnp.bfloat16

EPS = 1e-6
NEG_BIG = -1e30
CONV_WIDTH = 31
CONV_PAD = 32
N_META = 16
SSM_TILE_GROUPS = 8
VMEM_LIMIT = 58 * 1024 * 1024


def _cparams(sem):
    return pltpu.CompilerParams(dimension_semantics=sem, vmem_limit_bytes=VMEM_LIMIT)


def _sigmoid(x):
    return 1.0 / (1.0 + jnp.exp(-x))


def _silu(x):
    return x * _sigmoid(x)


def _gelu_tanh(x):
    return 0.5 * x * (1.0 + jnp.tanh(math.sqrt(2.0 / math.pi) * (x + 0.044715 * (x * x * x))))


def _row_chunk(rows):
    for c in (16, 8):
        if rows % c == 0:
            return c
    raise ValueError(f"row count {rows} must be a multiple of 8")


def _rms_rows(x_ref, g_ref, h_ref):
    rows = x_ref.shape[0]
    chunk = _row_chunk(rows)
    g = g_ref[...]

    def body(c, carry):
        r0 = pl.multiple_of(c * chunk, chunk)
        x = x_ref[pl.ds(r0, chunk), :]
        ms = jnp.mean(x * x, axis=-1, keepdims=True)
        h_ref[pl.ds(r0, chunk), :] = ((x * lax.rsqrt(ms + EPS)) * g).astype(h_ref.dtype)
        return carry

    lax.fori_loop(0, rows // chunk, body, 0)


def _ffn_kernel(x_hbm, g_ref, wg_ref, wu_ref, wd_ref, o_ref, h_ref, sem, *, bm):
    i = pl.program_id(0)
    f = pl.program_id(1)

    @pl.when(f == 0)
    def _():
        cp = pltpu.make_async_copy(x_hbm.at[pl.ds(i * bm, bm), :], o_ref, sem)
        cp.start()
        cp.wait()
        _rms_rows(o_ref, g_ref, h_ref)

    h = h_ref[...]
    gate = jnp.dot(h, wg_ref[...], preferred_element_type=F32)
    up = jnp.dot(h, wu_ref[...], preferred_element_type=F32)
    a = (0.5 * (_silu(gate) * up)).astype(BF16)
    o_ref[...] += jnp.dot(a, wd_ref[...], preferred_element_type=F32)


def _ffn(x, g, wg, wu, wd, *, bm, bf):
    m, d = x.shape
    f_dim = wg.shape[1]
    assert m % bm == 0 and f_dim % bf == 0
    return pl.pallas_call(
        functools.partial(_ffn_kernel, bm=bm),
        grid=(m // bm, f_dim // bf),
        in_specs=[
            pl.BlockSpec(memory_space=pl.ANY),
            pl.BlockSpec((1, d), lambda i, f: (0, 0)),
            pl.BlockSpec((d, bf), lambda i, f: (0, f)),
            pl.BlockSpec((d, bf), lambda i, f: (0, f)),
            pl.BlockSpec((bf, d), lambda i, f: (f, 0)),
        ],
        out_specs=pl.BlockSpec((bm, d), lambda i, f: (i, 0)),
        out_shape=jax.ShapeDtypeStruct((m, d), F32),
        scratch_shapes=[pltpu.VMEM((bm, d), BF16), pltpu.SemaphoreType.DMA(())],
        compiler_params=_cparams(("parallel", "arbitrary")),
        name="ffn",
    )(x, g, wg, wu, wd)


def _inproj_kernel(x_ref, g_ref, w_ref, o_ref, h_ref):
    @pl.when(pl.program_id(1) == 0)
    def _():
        _rms_rows(x_ref, g_ref, h_ref)

    o_ref[...] = jnp.dot(h_ref[...], w_ref[...], preferred_element_type=F32)


def _inproj(x, g, w, *, bm, bn):
    m, d = x.shape
    n = w.shape[1]
    assert m % bm == 0 and n % bn == 0
    return pl.pallas_call(
        _inproj_kernel,
        grid=(m // bm, n // bn),
        in_specs=[
            pl.BlockSpec((bm, d), lambda i, j: (i, 0)),
            pl.BlockSpec((1, d), lambda i, j: (0, 0)),
            pl.BlockSpec((d, bn), lambda i, j: (0, j)),
        ],
        out_specs=pl.BlockSpec((bm, bn), lambda i, j: (i, j)),
        out_shape=jax.ShapeDtypeStruct((m, n), F32),
        scratch_shapes=[pltpu.VMEM((bm, d), BF16)],
        compiler_params=_cparams(("parallel", "arbitrary")),
        name="inproj",
    )(x, g, w)


def _conv_seq_kernel(a_ref, gt_ref, st_ref, w_ref, y_ref, ns_ref, ext_ref, *, rows):
    t_len = a_ref.shape[0]
    n_chunks = t_len // rows
    ext_ref[0:CONV_PAD, :] = st_ref[...]

    def glu(c, carry):
        r0 = pl.multiple_of(c * rows, rows)
        ext_ref[pl.ds(CONV_PAD + r0, rows), :] = a_ref[pl.ds(r0, rows), :] * _sigmoid(gt_ref[pl.ds(r0, rows), :])
        return carry

    lax.fori_loop(0, n_chunks, glu, 0)
    w = w_ref[...]
    off = CONV_PAD - (CONV_WIDTH - 1)

    def conv(c, carry):
        r0 = pl.multiple_of(c * rows, rows)
        win = ext_ref[pl.ds(r0, rows + CONV_PAD), :]
        shifted = [win] + [win[r:r + rows + CONV_PAD - 8, :] for r in range(1, 8)]
        acc = None
        for k in range(CONV_WIDTH):
            s = off + k
            tap = w[k:k + 1, :] * shifted[s % 8][8 * (s // 8):8 * (s // 8) + rows, :]
            acc = tap if acc is None else acc + tap
        y_ref[pl.ds(r0, rows), :] = acc
        return carry

    lax.fori_loop(0, n_chunks, conv, 0)
    ns_ref[...] = ext_ref[pl.ds(t_len + off, CONV_WIDTH - 1), :]


def _conv_seq(z3, state_pad, conv_w, *, d_conv, cb, rows):
    b, t, _ = z3.shape
    nj = d_conv // cb
    return pl.pallas_call(
        functools.partial(_conv_seq_kernel, rows=rows),
        grid=(b, nj),
        in_specs=[
            pl.BlockSpec((None, t, cb), lambda i, j: (i, 0, j)),
            pl.BlockSpec((None, t, cb), lambda i, j: (i, 0, nj + j)),
            pl.BlockSpec((None, CONV_PAD, cb), lambda i, j: (i, 0, j)),
            pl.BlockSpec((CONV_WIDTH, cb), lambda i, j: (0, j)),
        ],
        out_specs=[
            pl.BlockSpec((None, t, cb), lambda i, j: (i, 0, j)),
            pl.BlockSpec((None, CONV_WIDTH - 1, cb), lambda i, j: (i, 0, j)),
        ],
        out_shape=[
            jax.ShapeDtypeStruct((b, t, d_conv), F32),
            jax.ShapeDtypeStruct((b, CONV_WIDTH - 1, d_conv), F32),
        ],
        scratch_shapes=[pltpu.VMEM((CONV_PAD + t, cb), F32)],
        compiler_params=_cparams(("parallel", "parallel")),
        name="conv_seq",
    )(z3, z3, state_pad, conv_w)


def _conv_step_kernel(a_ref, gt_ref, st_ref, w_ref, y_ref, ns_ref):
    u = a_ref[...] * _sigmoid(gt_ref[...])
    st = st_ref[...]
    w = w_ref[...]
    y_ref[...] = jnp.sum(st * w[None, 0:CONV_WIDTH - 1, :], axis=1) + u * w[CONV_WIDTH - 1:CONV_WIDTH, :]
    ns_ref[:, 0:CONV_WIDTH - 2, :] = st[:, 1:CONV_WIDTH - 1, :]
    ns_ref[:, CONV_WIDTH - 2:CONV_WIDTH - 1, :] = u[:, None, :]


def _conv_step(z, state, conv_w, *, d_conv):
    b = z.shape[0]
    return pl.pallas_call(
        _conv_step_kernel,
        grid=(1,),
        in_specs=[
            pl.BlockSpec((b, d_conv), lambda i: (0, 0)),
            pl.BlockSpec((b, d_conv), lambda i: (0, 1)),
            pl.BlockSpec((b, CONV_WIDTH - 1, d_conv), lambda i: (0, 0, 0)),
            pl.BlockSpec((CONV_WIDTH, d_conv), lambda i: (0, 0)),
        ],
        out_specs=[
            pl.BlockSpec((b, d_conv), lambda i: (0, 0)),
            pl.BlockSpec((b, CONV_WIDTH - 1, d_conv), lambda i: (0, 0, 0)),
        ],
        out_shape=[
            jax.ShapeDtypeStruct((b, d_conv), F32),
            jax.ShapeDtypeStruct((b, CONV_WIDTH - 1, d_conv), F32),
        ],
        compiler_params=_cparams(("arbitrary",)),
        name="conv_step",
    )(z, z, state, conv_w)


def _lambda(lq1_ref, lk1_ref, lq2_ref, lk2_ref, lam_init):
    s1 = jnp.sum(lq1_ref[...] * lk1_ref[...], axis=-1, keepdims=True)
    s2 = jnp.sum(lq2_ref[...] * lk2_ref[...], axis=-1, keepdims=True)
    return jnp.exp(s1) - jnp.exp(s2) + lam_init


def _subln(o, g, lam_init):
    ms = jnp.mean(o * o, axis=-1, keepdims=True)
    return ((o * lax.rsqrt(ms + EPS)) * g) * (1.0 - lam_init)


def _flash_kernel(q_ref, k_ref, v_ref, lq1_ref, lk1_ref, lq2_ref, lk2_ref, g_ref, o_ref,
                  m_ref, l_ref, acc_ref, *, tq, tk, hd, scale, lam_init):
    qi = pl.program_id(2)
    q = q_ref[...]
    qs = (q[:, :hd].astype(BF16), q[:, hd:].astype(BF16))
    m_ref[...] = jnp.full(m_ref.shape, NEG_BIG, F32)
    l_ref[...] = jnp.zeros(l_ref.shape, F32)
    acc_ref[...] = jnp.zeros(acc_ref.shape, F32)
    row = qi * tq + lax.broadcasted_iota(jnp.int32, (tq, tk), 0)
    col0 = lax.broadcasted_iota(jnp.int32, (tq, tk), 1)
    n_kv = (qi * tq + tq + tk - 1) // tk

    def body(j, carry):
        k0 = pl.multiple_of(j * tk, tk)
        k = k_ref[pl.ds(k0, tk), :]
        v = v_ref[pl.ds(k0, tk), :].astype(BF16)
        mask = (col0 + k0) <= row
        for c in range(2):
            kc = k[:, c * hd:(c + 1) * hd].astype(BF16)
            s = lax.dot_general(qs[c], kc, (((1,), (1,)), ((), ())), preferred_element_type=F32) * scale
            s = jnp.where(mask, s, NEG_BIG)
            m_old = m_ref[c]
            m_new = jnp.maximum(m_old, jnp.max(s, axis=-1, keepdims=True))
            alpha = jnp.exp(m_old - m_new)
            p = jnp.exp(s - m_new)
            l_ref[c] = alpha * l_ref[c] + jnp.sum(p, axis=-1, keepdims=True)
            acc_ref[c] = alpha * acc_ref[c] + jnp.dot(p.astype(BF16), v, preferred_element_type=F32)
            m_ref[c] = m_new
        return carry

    lax.fori_loop(0, n_kv, body, 0)
    lam = _lambda(lq1_ref, lk1_ref, lq2_ref, lk2_ref, lam_init)
    o = acc_ref[0] / l_ref[0] - lam * (acc_ref[1] / l_ref[1])
    o_ref[...] = _subln(o, g_ref[...], lam_init).astype(o_ref.dtype)


def _flash(z3, lq1, lk1, lq2, lk2, subln_g, *, n_heads, hd, q_col, k_col, v_col, tq, tk, lam_init):
    b, t, _ = z3.shape
    hw = 2 * hd
    vec = pl.BlockSpec((1, hd), lambda i, h, j: (0, 0))
    return pl.pallas_call(
        functools.partial(_flash_kernel, tq=tq, tk=tk, hd=hd, scale=1.0 / math.sqrt(hd), lam_init=lam_init),
        grid=(b, n_heads, t // tq),
        in_specs=[
            pl.BlockSpec((None, tq, hw), lambda i, h, j: (i, j, q_col + h)),
            pl.BlockSpec((None, t, hw), lambda i, h, j: (i, 0, k_col + h)),
            pl.BlockSpec((None, t, hw), lambda i, h, j: (i, 0, v_col + h)),
            vec, vec, vec, vec,
            pl.BlockSpec((1, hw), lambda i, h, j: (0, 0)),
        ],
        out_specs=pl.BlockSpec((None, tq, hw), lambda i, h, j: (i, j, h)),
        out_shape=jax.ShapeDtypeStruct((b, t, n_heads * hw), BF16),
        scratch_shapes=[pltpu.VMEM((2, tq, 1), F32), pltpu.VMEM((2, tq, 1), F32), pltpu.VMEM((2, tq, hw), F32)],
        compiler_params=_cparams(("parallel", "parallel", "arbitrary")),
        name="flash",
    )(z3, z3, z3, lq1, lk1, lq2, lk2, subln_g)


def _decode_update(k3, v3, q, ones, m_ref, l_ref, acc_ref, hd):
    n, n_heads, hw = k3.shape
    prod = (k3 * q[None]).reshape(n * n_heads, hw).astype(BF16)
    s = jnp.dot(prod, ones, preferred_element_type=F32).reshape(n, n_heads, hw) * (1.0 / math.sqrt(hd))
    m_old = m_ref[...]
    m_new = jnp.maximum(m_old, jnp.max(s, axis=0))
    alpha = jnp.exp(m_old - m_new)
    p = jnp.exp(s - m_new[None])
    l_ref[...] = alpha * l_ref[...] + jnp.sum(p, axis=0)
    m_ref[...] = m_new
    for c in range(2):
        pc = p[:, :, c * hd:(c + 1) * hd]
        pw = jnp.concatenate([pc, pc], axis=-1)
        ac = alpha[:, c * hd:(c + 1) * hd]
        acc_ref[c] = jnp.concatenate([ac, ac], axis=-1) * acc_ref[c] + jnp.sum(pw * v3, axis=0)


def _decode_kernel(pt_ref, q_ref, kc_ref, vc_ref, kn_ref, vn_ref, ones_ref, lq1_ref, lk1_ref, lq2_ref, lk2_ref,
                   g_ref, o_ref, m_ref, l_ref, acc_ref, *, hd, lam_init):
    del pt_ref
    p_idx = pl.program_id(1)

    @pl.when(p_idx == 0)
    def _():
        m_ref[...] = jnp.full(m_ref.shape, NEG_BIG, F32)
        l_ref[...] = jnp.zeros(l_ref.shape, F32)
        acc_ref[...] = jnp.zeros(acc_ref.shape, F32)

    q = q_ref[...]
    ones = ones_ref[...]
    _decode_update(kc_ref[...], vc_ref[...], q, ones, m_ref, l_ref, acc_ref, hd)

    @pl.when(p_idx == pl.num_programs(1) - 1)
    def _():
        _decode_update(kn_ref[...][None], vn_ref[...][None], q, ones, m_ref, l_ref, acc_ref, hd)
        lam = _lambda(lq1_ref, lk1_ref, lq2_ref, lk2_ref, lam_init)
        l = l_ref[...]
        l1 = jnp.concatenate([l[:, :hd], l[:, :hd]], axis=-1)
        l2 = jnp.concatenate([l[:, hd:], l[:, hd:]], axis=-1)
        o = acc_ref[0] / l1 - lam * (acc_ref[1] / l2)
        o_ref[...] = _subln(o, g_ref[...], lam_init)


def _decode_attn(page_table, q, cache_k, cache_v, k_new, v_new, lq1, lk1, lq2, lk2, subln_g, *, layer, lam_init):
    b, n_heads, hw = q.shape
    hd = hw // 2
    n_pages = page_table.shape[1]
    page = cache_k.shape[2]
    half = lax.broadcasted_iota(jnp.int32, (hw, hw), 0) // hd == lax.broadcasted_iota(jnp.int32, (hw, hw), 1) // hd
    ones = half.astype(BF16)
    per_seq = pl.BlockSpec((None, n_heads, hw), lambda i, p, pt: (i, 0, 0))
    paged = pl.BlockSpec((None, None, page, n_heads, hw), lambda i, p, pt: (layer, pt[i, p], 0, 0, 0))
    vec = pl.BlockSpec((1, hd), lambda i, p, pt: (0, 0))
    grid_spec = pltpu.PrefetchScalarGridSpec(
        num_scalar_prefetch=1,
        grid=(b, n_pages),
        in_specs=[per_seq, paged, paged, per_seq, per_seq,
                  pl.BlockSpec((hw, hw), lambda i, p, pt: (0, 0)),
                  vec, vec, vec, vec,
                  pl.BlockSpec((1, hw), lambda i, p, pt: (0, 0))],
        out_specs=per_seq,
        scratch_shapes=[pltpu.VMEM((n_heads, hw), F32), pltpu.VMEM((n_heads, hw), F32),
                        pltpu.VMEM((2, n_heads, hw), F32)],
    )
    return pl.pallas_call(
        functools.partial(_decode_kernel, hd=hd, lam_init=lam_init),
        grid_spec=grid_spec,
        out_shape=jax.ShapeDtypeStruct((b, n_heads, hw), F32),
        compiler_params=_cparams(("parallel", "arbitrary")),
        name="decode_attn",
    )(page_table, q, cache_k, cache_v, k_new, v_new, ones, lq1, lk1, lq2, lk2, subln_g)


def _ssm_prep_kernel(ar_ref, ai_ref, ldt_ref, br_ref, bi_ref, abr_ref, abi_ref, bbr_ref, bbi_ref):
    ar = ar_ref[...]
    ai = ai_ref[...]
    dt = jnp.exp(ldt_ref[...])
    mag = jnp.exp(ar * dt)
    ang = ai * dt
    abr = mag * jnp.cos(ang)
    abi = mag * jnp.sin(ang)
    den = ar * ar + ai * ai
    nr = abr - 1.0
    ni = abi
    f_re = (nr * ar + ni * ai) / den
    f_im = (ni * ar - nr * ai) / den
    br = br_ref[...]
    bi = bi_ref[...]
    abr_ref[...] = abr
    abi_ref[...] = abi
    bbr_ref[...] = f_re * br - f_im * bi
    bbi_ref[...] = f_re * bi + f_im * br


def _ssm_prep(a_re, a_im, log_dt, b_re, b_im):
    g, p = a_re.shape
    j = b_re.shape[2]
    s = g * p
    row = lambda x: x.reshape(1, s)
    bt = lambda x: x.reshape(s, j).T
    return pl.pallas_call(
        _ssm_prep_kernel,
        out_shape=[jax.ShapeDtypeStruct((1, s), F32), jax.ShapeDtypeStruct((1, s), F32),
                   jax.ShapeDtypeStruct((j, s), F32), jax.ShapeDtypeStruct((j, s), F32)],
        name="ssm_prep",
    )(row(a_re), row(a_im), row(jnp.broadcast_to(log_dt[:, None], (g, p))), bt(b_re), bt(b_im))


def _block_diag_in(bb, n_groups, p):
    j = bb.shape[0]
    tg = SSM_TILE_GROUPS
    nt = n_groups // tg
    x = bb.reshape(j, nt, tg, p).transpose(1, 0, 2, 3)
    eye = jnp.eye(tg, dtype=bb.dtype)
    x = eye[None, :, None, :, None] * x[:, None]
    return x.reshape(nt, tg * j, tg * p)


def _block_diag_out(c, n_groups):
    _, j, p = c.shape
    tg = SSM_TILE_GROUPS
    nt = n_groups // tg
    x = c.reshape(nt, tg, j, p).transpose(0, 1, 3, 2)
    eye = jnp.eye(tg, dtype=c.dtype)
    x = x[:, :, :, None, :] * eye[None, :, None, :, None]
    return x.reshape(nt, tg * p, tg * j)


def _ssm_in_proj(u, bre_ref, bim_ref, bur_ref, bui_ref):
    nt, kin, nout = bre_ref.shape
    for t in range(nt):
        ut = u[:, t * kin:(t + 1) * kin].astype(BF16)
        bur_ref[:, t * nout:(t + 1) * nout] = jnp.dot(ut, bre_ref[t], preferred_element_type=F32)
        bui_ref[:, t * nout:(t + 1) * nout] = jnp.dot(ut, bim_ref[t], preferred_element_type=F32)


def _ssm_out(xr_ref, xi_ref, u, cre_ref, cim_ref, d_ref, wglu_ref, o_ref):
    nt, kin, nout = cre_ref.shape
    parts = []
    for t in range(nt):
        xr = xr_ref[:, t * kin:(t + 1) * kin].astype(BF16)
        xi = xi_ref[:, t * kin:(t + 1) * kin].astype(BF16)
        parts.append(jnp.dot(xr, cre_ref[t], preferred_element_type=F32)
                     - jnp.dot(xi, cim_ref[t], preferred_element_type=F32))
    y = jnp.concatenate(parts, axis=-1) + d_ref[...] * u
    y = _gelu_tanh(y)
    y = y * _sigmoid(jnp.dot(y.astype(BF16), wglu_ref[...], preferred_element_type=F32))
    o_ref[...] = y.astype(o_ref.dtype)


def _ssm_seq_kernel(u_ref, abr_ref, abi_ref, bre_ref, bim_ref, cre_ref, cim_ref, d_ref, wglu_ref,
                    o_ref, sr_ref, si_ref, xr_ref, xi_ref, cr_ref, ci_ref, *, lane_chunk):
    c = pl.program_id(1)
    t_len = u_ref.shape[0]
    n_state = abr_ref.shape[1]

    @pl.when(c == 0)
    def _():
        cr_ref[...] = jnp.zeros(cr_ref.shape, F32)
        ci_ref[...] = jnp.zeros(ci_ref.shape, F32)

    u = u_ref[...]
    _ssm_in_proj(u, bre_ref, bim_ref, xr_ref, xi_ref)

    for lc in range(n_state // lane_chunk):
        sl = slice(lc * lane_chunk, (lc + 1) * lane_chunk)
        ar = abr_ref[:, sl]
        ai = abi_ref[:, sl]

        def step(t, carry, sl=sl, ar=ar, ai=ai):
            xr, xi = carry
            nr = ar * xr - ai * xi + xr_ref[pl.ds(t, 1), sl]
            ni = ar * xi + ai * xr + xi_ref[pl.ds(t, 1), sl]
            xr_ref[pl.ds(t, 1), sl] = nr
            xi_ref[pl.ds(t, 1), sl] = ni
            return nr, ni

        xr, xi = lax.fori_loop(0, t_len, step, (cr_ref[:, sl], ci_ref[:, sl]), unroll=8)
        cr_ref[:, sl] = xr
        ci_ref[:, sl] = xi

    _ssm_out(xr_ref, xi_ref, u, cre_ref, cim_ref, d_ref, wglu_ref, o_ref)

    @pl.when(c == pl.num_programs(1) - 1)
    def _():
        sr_ref[...] = cr_ref[...]
        si_ref[...] = ci_ref[...]


def _ssm_seq(z3, abr, abi, bre, bim, cre, cim, d, wglu, *, d_ssm, u_col, lc_rows):
    b, t, _ = z3.shape
    n_state = abr.shape[1]
    full2 = lambda a: pl.BlockSpec(a.shape, lambda i, c: (0, 0))
    full3 = lambda a: pl.BlockSpec(a.shape, lambda i, c: (0, 0, 0))
    return pl.pallas_call(
        functools.partial(_ssm_seq_kernel, lane_chunk=1024),
        grid=(b, t // lc_rows),
        in_specs=[pl.BlockSpec((None, lc_rows, d_ssm), lambda i, c: (i, c, u_col)),
                  full2(abr), full2(abi), full3(bre), full3(bim), full3(cre), full3(cim), full2(d), full2(wglu)],
        out_specs=[pl.BlockSpec((None, lc_rows, d_ssm), lambda i, c: (i, c, 0)),
                   pl.BlockSpec((None, 1, n_state), lambda i, c: (i, 0, 0)),
                   pl.BlockSpec((None, 1, n_state), lambda i, c: (i, 0, 0))],
        out_shape=[jax.ShapeDtypeStruct((b, t, d_ssm), BF16),
                   jax.ShapeDtypeStruct((b, 1, n_state), F32),
                   jax.ShapeDtypeStruct((b, 1, n_state), F32)],
        scratch_shapes=[pltpu.VMEM((lc_rows, n_state), F32), pltpu.VMEM((lc_rows, n_state), F32),
                        pltpu.VMEM((1, n_state), F32), pltpu.VMEM((1, n_state), F32)],
        compiler_params=_cparams(("parallel", "arbitrary")),
        name="ssm_seq",
    )(z3, abr, abi, bre, bim, cre, cim, d, wglu)


def _ssm_step_kernel(u_ref, hr_ref, hi_ref, abr_ref, abi_ref, bre_ref, bim_ref, cre_ref, cim_ref, d_ref, wglu_ref,
                     o_ref, sr_ref, si_ref):
    u = u_ref[...]
    _ssm_in_proj(u, bre_ref, bim_ref, sr_ref, si_ref)
    ar = abr_ref[...]
    ai = abi_ref[...]
    hr = hr_ref[...]
    hi = hi_ref[...]
    sr_ref[...] = ar * hr - ai * hi + sr_ref[...]
    si_ref[...] = ar * hi + ai * hr + si_ref[...]
    _ssm_out(sr_ref, si_ref, u, cre_ref, cim_ref, d_ref, wglu_ref, o_ref)


def _ssm_step(z, h_re, h_im, abr, abi, bre, bim, cre, cim, d, wglu, *, d_ssm, u_col):
    b = z.shape[0]
    n_state = abr.shape[1]
    full2 = lambda a: pl.BlockSpec(a.shape, lambda i: (0, 0))
    full3 = lambda a: pl.BlockSpec(a.shape, lambda i: (0, 0, 0))
    return pl.pallas_call(
        _ssm_step_kernel,
        grid=(1,),
        in_specs=[pl.BlockSpec((b, d_ssm), lambda i: (0, u_col)),
                  full2(h_re), full2(h_im), full2(abr), full2(abi),
                  full3(bre), full3(bim), full3(cre), full3(cim), full2(d), full2(wglu)],
        out_specs=[pl.BlockSpec((b, d_ssm), lambda i: (0, 0)),
                   pl.BlockSpec((b, n_state), lambda i: (0, 0)),
                   pl.BlockSpec((b, n_state), lambda i: (0, 0))],
        out_shape=[jax.ShapeDtypeStruct((b, d_ssm), BF16),
                   jax.ShapeDtypeStruct((b, n_state), F32),
                   jax.ShapeDtypeStruct((b, n_state), F32)],
        compiler_params=_cparams(("arbitrary",)),
        name="ssm_step",
    )(z, h_re, h_im, abr, abi, bre, bim, cre, cim, d, wglu)


def _merge_kernel(yc_ref, att_ref, ys_ref, ga_ref, gb_ref, gc_ref, cb_ref, lg_ref, lb_ref,
                  wc_ref, wa_ref, ws_ref, o_ref, hc_ref):
    @pl.when(pl.program_id(1) == 0)
    def _():
        rows = yc_ref.shape[0]
        chunk = _row_chunk(rows)
        bias = cb_ref[...]
        lg = lg_ref[...]
        lb = lb_ref[...]

        def body(c, carry):
            r0 = pl.multiple_of(c * chunk, chunk)
            y = yc_ref[pl.ds(r0, chunk), :] + bias
            mu = jnp.mean(y, axis=-1, keepdims=True)
            yc = y - mu
            var = jnp.mean(yc * yc, axis=-1, keepdims=True)
            hc_ref[pl.ds(r0, chunk), :] = _silu((yc * lax.rsqrt(var + EPS)) * lg + lb).astype(BF16)
            return carry

        lax.fori_loop(0, rows // chunk, body, 0)

    a = jnp.dot(hc_ref[...], wc_ref[...], preferred_element_type=F32)
    b = jnp.dot(att_ref[...].astype(BF16), wa_ref[...], preferred_element_type=F32)
    c = jnp.dot(ys_ref[...], ws_ref[...], preferred_element_type=F32)
    merged = _sigmoid(ga_ref[...]) * a + _sigmoid(gb_ref[...]) * b + _sigmoid(gc_ref[...]) * c
    o_ref[...] = merged.astype(o_ref.dtype)


def _merge(yconv, att, ys, z, conv_b, ln_g, ln_b, w_conv, w_attn, w_ssm, *, gate_col, bm, bn):
    m, d_conv = yconv.shape
    d_attn = att.shape[1]
    d_ssm = ys.shape[1]
    d = w_conv.shape[1]
    assert m % bm == 0 and d % bn == 0 and gate_col % bn == 0
    g0 = gate_col // bn
    nj = d // bn
    vec = pl.BlockSpec((1, d_conv), lambda i, j: (0, 0))
    return pl.pallas_call(
        _merge_kernel,
        grid=(m // bm, nj),
        in_specs=[
            pl.BlockSpec((bm, d_conv), lambda i, j: (i, 0)),
            pl.BlockSpec((bm, d_attn), lambda i, j: (i, 0)),
            pl.BlockSpec((bm, d_ssm), lambda i, j: (i, 0)),
            pl.BlockSpec((bm, bn), lambda i, j: (i, g0 + j)),
            pl.BlockSpec((bm, bn), lambda i, j: (i, g0 + nj + j)),
            pl.BlockSpec((bm, bn), lambda i, j: (i, g0 + 2 * nj + j)),
            vec, vec, vec,
            pl.BlockSpec((d_conv, bn), lambda i, j: (0, j)),
            pl.BlockSpec((d_attn, bn), lambda i, j: (0, j)),
            pl.BlockSpec((d_ssm, bn), lambda i, j: (0, j)),
        ],
        out_specs=pl.BlockSpec((bm, bn), lambda i, j: (i, j)),
        out_shape=jax.ShapeDtypeStruct((m, d), BF16),
        scratch_shapes=[pltpu.VMEM((bm, d_conv), BF16)],
        compiler_params=_cparams(("parallel", "arbitrary")),
        name="merge",
    )(yconv, att, ys, z, z, z, conv_b, ln_g, ln_b, w_conv, w_attn, w_ssm)


def _resproj_kernel(a_ref, w_ref, x_ref, o_ref):
    o_ref[...] = x_ref[...] + jnp.dot(a_ref[...], w_ref[...], preferred_element_type=F32)


def _resproj(a, w, x, *, bm, bn):
    m, k = a.shape
    n = w.shape[1]
    assert m % bm == 0 and n % bn == 0
    return pl.pallas_call(
        _resproj_kernel,
        grid=(m // bm, n // bn),
        in_specs=[
            pl.BlockSpec((bm, k), lambda i, j: (i, 0)),
            pl.BlockSpec((k, bn), lambda i, j: (0, j)),
            pl.BlockSpec((bm, bn), lambda i, j: (i, j)),
        ],
        out_specs=pl.BlockSpec((bm, bn), lambda i, j: (i, j)),
        out_shape=jax.ShapeDtypeStruct((m, n), F32),
        compiler_params=_cparams(("parallel", "arbitrary")),
        name="resproj",
    )(a, w, x)


def _final_norm_kernel(x_ref, g_ref, o_ref):
    _rms_rows(x_ref, g_ref, o_ref)


def _final_norm(x, g, *, bm):
    m, d = x.shape
    assert m % bm == 0
    return pl.pallas_call(
        _final_norm_kernel,
        grid=(m // bm,),
        in_specs=[pl.BlockSpec((bm, d), lambda i: (i, 0)), pl.BlockSpec((1, d), lambda i: (0, 0))],
        out_specs=pl.BlockSpec((bm, d), lambda i: (i, 0)),
        out_shape=jax.ShapeDtypeStruct((m, d), F32),
        compiler_params=_cparams(("parallel",)),
        name="final_norm",
    )(x, g)


def kernel(x_prompt, x_sample, cache_k, cache_v, page_table, state_conv, state_ssm_re, state_ssm_im, meta_tokens, norm_ffn1, ffn1_gate, ffn1_up, ffn1_down, norm_mix, w_in, conv_w, conv_b, conv_ln_g, conv_ln_b, conv_proj, lambda_q1, lambda_k1, lambda_q2, lambda_k2, attn_subln_g, attn_proj, ssm_a_re, ssm_a_im, ssm_log_dt, ssm_b_re, ssm_b_im, ssm_c_re, ssm_c_im, ssm_d, ssm_glu, ssm_proj, w_out, norm_ffn2, ffn2_gate, ffn2_up, ffn2_down, norm_final):
    bp, seq, d = x_prompt.shape
    db = x_sample.shape[0]
    depth = w_in.shape[0]
    d_conv = conv_w.shape[2]
    hd = lambda_q1.shape[1]
    hw = 2 * hd
    n_heads = cache_k.shape[3]
    d_attn = n_heads * hw
    n_groups, n_state_g = ssm_a_re.shape[1], ssm_a_re.shape[2]
    d_ssm = ssm_d.shape[1]
    n_state = n_groups * n_state_g
    t = seq + N_META
    mp = bp * t
    q_off = 2 * d_conv
    k_off = q_off + d_attn
    v_off = k_off + d_attn
    u_off = v_off + d_attn
    gate_off = u_off + d_ssm

    bm_p = t // 3
    bf_ffn = 256
    row2 = lambda v: v.reshape(1, -1)

    xp = jnp.concatenate([jnp.broadcast_to(meta_tokens[None], (bp, N_META, d)), x_prompt], axis=1).reshape(mp, d)
    xs = x_sample.reshape(db, d)
    zero_conv = jnp.zeros((bp, CONV_PAD, d_conv), F32)

    outs_p = {k: [] for k in ("k", "v", "conv", "sre", "sim")}
    outs_s = {k: [] for k in ("k", "v", "conv", "sre", "sim")}
    for l in range(depth):
        lam_init = 0.8 - 0.6 * math.exp(-0.3 * l)
        wg1, wu1, wd1 = ffn1_gate[l].astype(BF16), ffn1_up[l].astype(BF16), ffn1_down[l].astype(BF16)
        wg2, wu2, wd2 = ffn2_gate[l].astype(BF16), ffn2_up[l].astype(BF16), ffn2_down[l].astype(BF16)
        win = w_in[l].astype(BF16)
        wout = w_out[l].astype(BF16)
        wconv, wattn, wssm = conv_proj[l].astype(BF16), attn_proj[l].astype(BF16), ssm_proj[l].astype(BF16)
        wglu = ssm_glu[l].astype(BF16)
        lq1, lk1, lq2, lk2 = row2(lambda_q1[l]), row2(lambda_k1[l]), row2(lambda_q2[l]), row2(lambda_k2[l])
        sub_g = row2(attn_subln_g[l])

        abr, abi, bbr, bbi = _ssm_prep(ssm_a_re[l], ssm_a_im[l], ssm_log_dt[l], ssm_b_re[l], ssm_b_im[l])
        bre = _block_diag_in(bbr, n_groups, n_state_g).astype(BF16)
        bim = _block_diag_in(bbi, n_groups, n_state_g).astype(BF16)
        cre = _block_diag_out(ssm_c_re[l], n_groups).astype(BF16)
        cim = _block_diag_out(ssm_c_im[l], n_groups).astype(BF16)
        dvec = row2(ssm_d[l])

        xp = _ffn(xp, row2(norm_ffn1[l]), wg1, wu1, wd1, bm=bm_p, bf=bf_ffn)
        zp = _inproj(xp, row2(norm_mix[l]), win, bm=bm_p, bn=512)
        zp3 = zp.reshape(bp, t, -1)
        yconv_p, conv_p = _conv_seq(zp3, zero_conv, conv_w[l], d_conv=d_conv, cb=256, rows=48)
        att_p = _flash(zp3, lq1, lk1, lq2, lk2, sub_g, n_heads=n_heads, hd=hd, q_col=q_off // hw,
                       k_col=k_off // hw, v_col=v_off // hw, tq=t // 3, tk=t // 3, lam_init=lam_init)
        ys_p, sre_p, sim_p = _ssm_seq(zp3, abr, abi, bre, bim, cre, cim, dvec, wglu,
                                      d_ssm=d_ssm, u_col=u_off // d_ssm, lc_rows=t // 3)
        merged_p = _merge(yconv_p.reshape(mp, d_conv), att_p.reshape(mp, d_attn), ys_p.reshape(mp, d_ssm), zp,
                          row2(conv_b[l]), row2(conv_ln_g[l]), row2(conv_ln_b[l]), wconv, wattn, wssm,
                          gate_col=gate_off, bm=bm_p, bn=512)
        xp = _resproj(merged_p, wout, xp, bm=bm_p, bn=512)
        xp = _ffn(xp, row2(norm_ffn2[l]), wg2, wu2, wd2, bm=bm_p, bf=bf_ffn)
        outs_p["k"].append(zp3[:, :, k_off:v_off].reshape(bp, t, n_heads, hw))
        outs_p["v"].append(zp3[:, :, v_off:u_off].reshape(bp, t, n_heads, hw))
        outs_p["conv"].append(conv_p)
        outs_p["sre"].append(sre_p.reshape(bp, n_groups, n_state_g))
        outs_p["sim"].append(sim_p.reshape(bp, n_groups, n_state_g))

        xs = _ffn(xs, row2(norm_ffn1[l]), wg1, wu1, wd1, bm=db, bf=bf_ffn)
        zs = _inproj(xs, row2(norm_mix[l]), win, bm=db, bn=512)
        yconv_s, conv_s = _conv_step(zs, state_conv[l], conv_w[l], d_conv=d_conv)
        q_s = zs[:, q_off:k_off].reshape(db, n_heads, hw)
        k_s = zs[:, k_off:v_off].reshape(db, n_heads, hw)
        v_s = zs[:, v_off:u_off].reshape(db, n_heads, hw)
        att_s = _decode_attn(page_table, q_s, cache_k, cache_v, k_s, v_s, lq1, lk1, lq2, lk2, sub_g,
                             layer=l, lam_init=lam_init)
        ys_s, sre_s, sim_s = _ssm_step(zs, state_ssm_re[l].reshape(db, n_state), state_ssm_im[l].reshape(db, n_state),
                                       abr, abi, bre, bim, cre, cim, dvec, wglu, d_ssm=d_ssm, u_col=u_off // d_ssm)
        merged_s = _merge(yconv_s, att_s.reshape(db, d_attn), ys_s, zs,
                          row2(conv_b[l]), row2(conv_ln_g[l]), row2(conv_ln_b[l]), wconv, wattn, wssm,
                          gate_col=gate_off, bm=db, bn=512)
        xs = _resproj(merged_s, wout, xs, bm=db, bn=512)
        xs = _ffn(xs, row2(norm_ffn2[l]), wg2, wu2, wd2, bm=db, bf=bf_ffn)
        outs_s["k"].append(k_s.reshape(db, 1, n_heads, hw))
        outs_s["v"].append(v_s.reshape(db, 1, n_heads, hw))
        outs_s["conv"].append(conv_s)
        outs_s["sre"].append(sre_s.reshape(db, n_groups, n_state_g))
        outs_s["sim"].append(sim_s.reshape(db, n_groups, n_state_g))

    y_prompt = _final_norm(xp, row2(norm_final), bm=bm_p).reshape(bp, t, d)[:, N_META:]
    y_sample = _final_norm(xs, row2(norm_final), bm=db).reshape(db, 1, d)
    st = jnp.stack
    return (y_prompt, y_sample, st(outs_p["k"]), st(outs_p["v"]), st(outs_p["conv"]), st(outs_p["sre"]),
            st(outs_p["sim"]), st(outs_s["k"]), st(outs_s["v"]), st(outs_s["conv"]), st(outs_s["sre"]),
            st(outs_s["sim"]))
```

```python
import functools
import math

import jax
import jax.numpy as jnp
from jax import lax
from jax.experimental import pallas as pl
from jax.experimental.pallas import tpu as pltpu

F32 = jnp.float32
BF16 = jnp.bfloat16

EPS = 1e-6
NEG_BIG = -1e30
LOG2_E = 1.4426950408889634
CONV_WIDTH = 31
CONV_PAD = 32
N_META = 16
SSM_TILE_GROUPS = 8
FFN_BF = 256
FFN_CAST_BF = 256
DECODE_PAGES = 4
VMEM_LIMIT = 60 * 1024 * 1024


def _cparams(sem):
    return pltpu.CompilerParams(dimension_semantics=sem, vmem_limit_bytes=VMEM_LIMIT)


def _sigmoid(x):
    return 1.0 / (1.0 + jnp.exp(-x))


def _silu(x):
    return x * _sigmoid(x)


def _gelu_tanh(x):
    return 0.5 * x * (1.0 + jnp.tanh(math.sqrt(2.0 / math.pi) * (x + 0.044715 * (x * x * x))))


def _row_chunk(rows):
    for c in (16, 8):
        if rows % c == 0:
            return c
    raise ValueError(f"row count {rows} must be a multiple of 8")


def _rms_rows(x_ref, g_ref, h_ref):
    rows = x_ref.shape[0]
    chunk = _row_chunk(rows)
    g = g_ref[...]

    def body(c, carry):
        r0 = pl.multiple_of(c * chunk, chunk)
        x = x_ref[pl.ds(r0, chunk), :]
        ms = jnp.mean(x * x, axis=-1, keepdims=True)
        h_ref[pl.ds(r0, chunk), :] = ((x * lax.rsqrt(ms + EPS)) * g).astype(h_ref.dtype)
        return carry

    lax.fori_loop(0, rows // chunk, body, 0, unroll=2)


def _row_splits(rows, parts):
    unit = _row_chunk(rows)
    n_units = rows // unit
    parts = min(parts, n_units)
    base, extra = divmod(n_units, parts)
    out, r0 = [], 0
    for k in range(parts):
        n = (base + (1 if k < extra else 0)) * unit
        out.append((r0, n))
        r0 += n
    return tuple(out)


def _ffn_math(h, wg, wu, wd):
    gate = jnp.dot(h, wg, preferred_element_type=F32)
    up = jnp.dot(h, wu, preferred_element_type=F32)
    a = (0.5 * (_silu(gate) * up)).astype(BF16)
    return jnp.dot(a, wd, preferred_element_type=F32)


def _ffn_kernel(x_hbm, g_ref, wg_ref, wu_ref, wd_ref, o_ref, h_ref, sem, *, bm, splits):
    i = pl.program_id(0)
    f = pl.program_id(1)

    @pl.when(f == 0)
    def _():
        copies = [pltpu.make_async_copy(x_hbm.at[pl.ds(i * bm + r0, n), :], o_ref.at[pl.ds(r0, n), :], sem.at[k])
                  for k, (r0, n) in enumerate(splits)]
        for cp in copies:
            cp.start()
        for cp, (r0, n) in zip(copies, splits):
            cp.wait()
            _rms_rows(o_ref.at[pl.ds(r0, n), :], g_ref, h_ref.at[pl.ds(r0, n), :])

    o_ref[...] += _ffn_math(h_ref[...], wg_ref[...], wu_ref[...], wd_ref[...])


def _ffn(x, g, wg, wu, wd, *, bm, bf):
    m, d = x.shape
    f_dim = wg.shape[1]
    assert m % bm == 0 and f_dim % bf == 0
    splits = _row_splits(bm, 4)
    return pl.pallas_call(
        functools.partial(_ffn_kernel, bm=bm, splits=splits),
        grid=(m // bm, f_dim // bf),
        in_specs=[
            pl.BlockSpec(memory_space=pl.ANY),
            pl.BlockSpec((1, d), lambda i, f: (0, 0)),
            pl.BlockSpec((d, bf), lambda i, f: (0, f)),
            pl.BlockSpec((d, bf), lambda i, f: (0, f)),
            pl.BlockSpec((bf, d), lambda i, f: (f, 0)),
        ],
        out_specs=pl.BlockSpec((bm, d), lambda i, f: (i, 0)),
        out_shape=jax.ShapeDtypeStruct((m, d), F32),
        scratch_shapes=[pltpu.VMEM((bm, d), BF16), pltpu.SemaphoreType.DMA((len(splits),))],
        compiler_params=_cparams(("parallel", "arbitrary")),
        name="ffn",
    )(x, g, wg, wu, wd)


def _ffn_cast_kernel(x_ref, g_ref, wg_ref, wu_ref, wd_ref, o_ref, wgb_ref, wub_ref, wdb_ref, h_ref):
    @pl.when(pl.program_id(0) == 0)
    def _():
        _rms_rows(x_ref, g_ref, h_ref)
        o_ref[...] = x_ref[...]

    wgb_ref[...] = wg_ref[...].astype(BF16)
    wub_ref[...] = wu_ref[...].astype(BF16)
    wdb_ref[...] = wd_ref[...].astype(BF16)
    o_ref[...] += _ffn_math(h_ref[...], wgb_ref[...], wub_ref[...], wdb_ref[...])


def _ffn_cast(x, g, wg, wu, wd, *, layer, bf):
    m, d = x.shape
    f_dim = wg.shape[2]
    assert f_dim % bf == 0
    return pl.pallas_call(
        _ffn_cast_kernel,
        grid=(f_dim // bf,),
        in_specs=[
            pl.BlockSpec((m, d), lambda f: (0, 0)),
            pl.BlockSpec((1, d), lambda f: (0, 0)),
            pl.BlockSpec((None, d, bf), lambda f: (layer, 0, f)),
            pl.BlockSpec((None, d, bf), lambda f: (layer, 0, f)),
            pl.BlockSpec((None, bf, d), lambda f: (layer, f, 0)),
        ],
        out_specs=[
            pl.BlockSpec((m, d), lambda f: (0, 0)),
            pl.BlockSpec((d, bf), lambda f: (0, f)),
            pl.BlockSpec((d, bf), lambda f: (0, f)),
            pl.BlockSpec((bf, d), lambda f: (f, 0)),
        ],
        out_shape=[
            jax.ShapeDtypeStruct((m, d), F32),
            jax.ShapeDtypeStruct((d, f_dim), BF16),
            jax.ShapeDtypeStruct((d, f_dim), BF16),
            jax.ShapeDtypeStruct((f_dim, d), BF16),
        ],
        scratch_shapes=[pltpu.VMEM((m, d), BF16)],
        compiler_params=_cparams(("arbitrary",)),
        name="ffn_cast",
    )(x, g, wg, wu, wd)


def _inproj_kernel(x_ref, g_ref, w_ref, o_ref, h_ref):
    @pl.when(pl.program_id(1) == 0)
    def _():
        _rms_rows(x_ref, g_ref, h_ref)

    o_ref[...] = jnp.dot(h_ref[...], w_ref[...], preferred_element_type=F32)


def _inproj(x, g, w, *, bm, bn):
    m, d = x.shape
    n = w.shape[1]
    assert m % bm == 0 and n % bn == 0
    return pl.pallas_call(
        _inproj_kernel,
        grid=(m // bm, n // bn),
        in_specs=[
            pl.BlockSpec((bm, d), lambda i, j: (i, 0)),
            pl.BlockSpec((1, d), lambda i, j: (0, 0)),
            pl.BlockSpec((d, bn), lambda i, j: (0, j)),
        ],
        out_specs=pl.BlockSpec((bm, bn), lambda i, j: (i, j)),
        out_shape=jax.ShapeDtypeStruct((m, n), F32),
        scratch_shapes=[pltpu.VMEM((bm, d), BF16)],
        compiler_params=_cparams(("parallel", "arbitrary")),
        name="inproj",
    )(x, g, w)


def _inproj_cast_kernel(x_ref, g_ref, w_ref, o_ref, wb_ref, h_ref):
    @pl.when(pl.program_id(0) == 0)
    def _():
        _rms_rows(x_ref, g_ref, h_ref)

    wb_ref[...] = w_ref[...].astype(BF16)
    o_ref[...] = jnp.dot(h_ref[...], wb_ref[...], preferred_element_type=F32)


def _inproj_cast(x, g, w, *, layer, bn):
    m, d = x.shape
    n = w.shape[2]
    assert n % bn == 0
    return pl.pallas_call(
        _inproj_cast_kernel,
        grid=(n // bn,),
        in_specs=[
            pl.BlockSpec((m, d), lambda j: (0, 0)),
            pl.BlockSpec((1, d), lambda j: (0, 0)),
            pl.BlockSpec((None, d, bn), lambda j: (layer, 0, j)),
        ],
        out_specs=[pl.BlockSpec((m, bn), lambda j: (0, j)), pl.BlockSpec((d, bn), lambda j: (0, j))],
        out_shape=[jax.ShapeDtypeStruct((m, n), F32), jax.ShapeDtypeStruct((d, n), BF16)],
        scratch_shapes=[pltpu.VMEM((m, d), BF16)],
        compiler_params=_cparams(("arbitrary",)),
        name="inproj_cast",
    )(x, g, w)


def _conv_seq_kernel(a_ref, gt_ref, st_ref, w_ref, y_ref, ns_ref, ext_ref, *, rows):
    t_len = a_ref.shape[0]
    n_chunks = t_len // rows
    ext_ref[0:CONV_PAD, :] = st_ref[...]

    def glu(c, carry):
        r0 = pl.multiple_of(c * rows, rows)
        ext_ref[pl.ds(CONV_PAD + r0, rows), :] = a_ref[pl.ds(r0, rows), :] * _sigmoid(gt_ref[pl.ds(r0, rows), :])
        return carry

    lax.fori_loop(0, n_chunks, glu, 0)
    w = w_ref[...]
    off = CONV_PAD - (CONV_WIDTH - 1)

    def conv(c, carry):
        r0 = pl.multiple_of(c * rows, rows)
        win = ext_ref[pl.ds(r0, rows + CONV_PAD), :]
        shifted = [win] + [win[r:r + rows + CONV_PAD - 8, :] for r in range(1, 8)]
        acc = None
        for k in range(CONV_WIDTH):
            s = off + k
            tap = w[k:k + 1, :] * shifted[s % 8][8 * (s // 8):8 * (s // 8) + rows, :]
            acc = tap if acc is None else acc + tap
        y_ref[pl.ds(r0, rows), :] = acc
        return carry

    lax.fori_loop(0, n_chunks, conv, 0)
    ns_ref[...] = ext_ref[pl.ds(t_len + off, CONV_WIDTH - 1), :]


def _conv_seq(z3, state_pad, conv_w, *, d_conv, cb, rows):
    b, t, _ = z3.shape
    nj = d_conv // cb
    return pl.pallas_call(
        functools.partial(_conv_seq_kernel, rows=rows),
        grid=(b, nj),
        in_specs=[
            pl.BlockSpec((None, t, cb), lambda i, j: (i, 0, j)),
            pl.BlockSpec((None, t, cb), lambda i, j: (i, 0, nj + j)),
            pl.BlockSpec((None, CONV_PAD, cb), lambda i, j: (i, 0, j)),
            pl.BlockSpec((CONV_WIDTH, cb), lambda i, j: (0, j)),
        ],
        out_specs=[
            pl.BlockSpec((None, t, cb), lambda i, j: (i, 0, j)),
            pl.BlockSpec((None, CONV_WIDTH - 1, cb), lambda i, j: (i, 0, j)),
        ],
        out_shape=[
            jax.ShapeDtypeStruct((b, t, d_conv), F32),
            jax.ShapeDtypeStruct((b, CONV_WIDTH - 1, d_conv), F32),
        ],
        scratch_shapes=[pltpu.VMEM((CONV_PAD + t, cb), F32)],
        compiler_params=_cparams(("parallel", "parallel")),
        name="conv_seq",
    )(z3, z3, state_pad, conv_w)


def _conv_step_kernel(a_ref, gt_ref, st_ref, w_ref, y_ref, ns_ref):
    u = a_ref[...] * _sigmoid(gt_ref[...])
    st = st_ref[...]
    w = w_ref[...]
    y_ref[...] = jnp.sum(st * w[None, 0:CONV_WIDTH - 1, :], axis=1) + u * w[CONV_WIDTH - 1:CONV_WIDTH, :]
    ns_ref[:, 0:CONV_WIDTH - 2, :] = st[:, 1:CONV_WIDTH - 1, :]
    ns_ref[:, CONV_WIDTH - 2:CONV_WIDTH - 1, :] = u[:, None, :]


def _conv_step(z, state, conv_w, *, d_conv):
    b = z.shape[0]
    return pl.pallas_call(
        _conv_step_kernel,
        grid=(1,),
        in_specs=[
            pl.BlockSpec((b, d_conv), lambda i: (0, 0)),
            pl.BlockSpec((b, d_conv), lambda i: (0, 1)),
            pl.BlockSpec((b, CONV_WIDTH - 1, d_conv), lambda i: (0, 0, 0)),
            pl.BlockSpec((CONV_WIDTH, d_conv), lambda i: (0, 0)),
        ],
        out_specs=[
            pl.BlockSpec((b, d_conv), lambda i: (0, 0)),
            pl.BlockSpec((b, CONV_WIDTH - 1, d_conv), lambda i: (0, 0, 0)),
        ],
        out_shape=[
            jax.ShapeDtypeStruct((b, d_conv), F32),
            jax.ShapeDtypeStruct((b, CONV_WIDTH - 1, d_conv), F32),
        ],
        compiler_params=_cparams(("arbitrary",)),
        name="conv_step",
    )(z, z, state, conv_w)


def _lambda(lq1_ref, lk1_ref, lq2_ref, lk2_ref, lam_init):
    s1 = jnp.sum(lq1_ref[...] * lk1_ref[...], axis=-1, keepdims=True)
    s2 = jnp.sum(lq2_ref[...] * lk2_ref[...], axis=-1, keepdims=True)
    return jnp.exp(s1) - jnp.exp(s2) + lam_init


def _subln(o, g, lam_init):
    ms = jnp.mean(o * o, axis=-1, keepdims=True)
    return ((o * lax.rsqrt(ms + EPS)) * g) * (1.0 - lam_init)


def _flash_kernel(q_ref, k_ref, v_ref, lq1_ref, lk1_ref, lq2_ref, lk2_ref, g_ref, o_ref, ko_ref, vo_ref,
                  m_ref, l_ref, acc_ref, *, tq, tk, hd, scale, lam_init):
    qi = pl.program_id(2)

    @pl.when(qi == 0)
    def _():
        ko_ref[...] = k_ref[...]
        vo_ref[...] = v_ref[...]

    q = q_ref[...] * (LOG2_E * scale)
    qs = (q[:, :hd].astype(BF16), q[:, hd:].astype(BF16))
    m_ref[...] = jnp.full(m_ref.shape, NEG_BIG, F32)
    l_ref[...] = jnp.zeros(l_ref.shape, F32)
    acc_ref[...] = jnp.zeros(acc_ref.shape, F32)
    row = qi * tq + lax.broadcasted_iota(jnp.int32, (tq, tk), 0)
    col0 = lax.broadcasted_iota(jnp.int32, (tq, tk), 1)
    n_kv = (qi * tq + tq + tk - 1) // tk
    n_full = (qi * tq + 1) // tk

    def body(j, carry, *, masked):
        k0 = pl.multiple_of(j * tk, tk)
        k = k_ref[pl.ds(k0, tk), :]
        v = v_ref[pl.ds(k0, tk), :].astype(BF16)
        for c in range(2):
            kc = k[:, c * hd:(c + 1) * hd].astype(BF16)
            s = lax.dot_general(qs[c], kc, (((1,), (1,)), ((), ())), preferred_element_type=F32)
            if masked:
                s = jnp.where((col0 + k0) <= row, s, NEG_BIG)
            m_old = m_ref[c]
            m_new = jnp.maximum(m_old, jnp.max(s, axis=-1, keepdims=True))
            alpha = jnp.exp2(m_old - m_new)
            p = jnp.exp2(s - m_new)
            l_ref[c] = alpha * l_ref[c] + jnp.sum(p, axis=-1, keepdims=True)
            acc_ref[c] = alpha * acc_ref[c] + jnp.dot(p.astype(BF16), v, preferred_element_type=F32)
            m_ref[c] = m_new
        return carry

    lax.fori_loop(0, n_full, functools.partial(body, masked=False), 0)
    lax.fori_loop(n_full, n_kv, functools.partial(body, masked=True), 0)
    lam = _lambda(lq1_ref, lk1_ref, lq2_ref, lk2_ref, lam_init)
    o = acc_ref[0] / l_ref[0] - lam * (acc_ref[1] / l_ref[1])
    o_ref[...] = _subln(o, g_ref[...], lam_init).astype(o_ref.dtype)


def _flash(z3, lq1, lk1, lq2, lk2, subln_g, *, n_heads, hd, q_col, k_col, v_col, tq, tk, lam_init):
    b, t, _ = z3.shape
    hw = 2 * hd
    vec = pl.BlockSpec((1, hd), lambda i, h, j: (0, 0))
    slab = pl.BlockSpec((None, t, hw), lambda i, h, j: (i, 0, h))
    kv_shape = jax.ShapeDtypeStruct((b, t, n_heads * hw), F32)
    return pl.pallas_call(
        functools.partial(_flash_kernel, tq=tq, tk=tk, hd=hd, scale=1.0 / math.sqrt(hd), lam_init=lam_init),
        grid=(b, n_heads, t // tq),
        in_specs=[
            pl.BlockSpec((None, tq, hw), lambda i, h, j: (i, j, q_col + h)),
            pl.BlockSpec((None, t, hw), lambda i, h, j: (i, 0, k_col + h)),
            pl.BlockSpec((None, t, hw), lambda i, h, j: (i, 0, v_col + h)),
            vec, vec, vec, vec,
            pl.BlockSpec((1, hw), lambda i, h, j: (0, 0)),
        ],
        out_specs=[pl.BlockSpec((None, tq, hw), lambda i, h, j: (i, j, h)), slab, slab],
        out_shape=[jax.ShapeDtypeStruct((b, t, n_heads * hw), BF16), kv_shape, kv_shape],
        scratch_shapes=[pltpu.VMEM((2, tq, 1), F32), pltpu.VMEM((2, tq, 1), F32), pltpu.VMEM((2, tq, hw), F32)],
        compiler_params=_cparams(("parallel", "parallel", "arbitrary")),
        name="flash",
    )(z3, z3, z3, lq1, lk1, lq2, lk2, subln_g)


def _decode_update(k3, v3, q, ones, m_ref, l_ref, acc_ref, hd):
    n, n_heads, hw = k3.shape
    qs = q * (LOG2_E / math.sqrt(hd))
    prod = (k3 * qs[None]).reshape(n * n_heads, hw).astype(BF16)
    s = jnp.dot(prod, ones, preferred_element_type=F32).reshape(n, n_heads, hw)
    m_old = m_ref[...]
    m_new = jnp.maximum(m_old, jnp.max(s, axis=0))
    alpha = jnp.exp2(m_old - m_new)
    p = jnp.exp2(s - m_new[None])
    l_ref[...] = alpha * l_ref[...] + jnp.sum(p, axis=0)
    m_ref[...] = m_new
    for c in range(2):
        pc = p[:, :, c * hd:(c + 1) * hd]
        pw = jnp.concatenate([pc, pc], axis=-1)
        ac = alpha[:, c * hd:(c + 1) * hd]
        acc_ref[c] = jnp.concatenate([ac, ac], axis=-1) * acc_ref[c] + jnp.sum(pw * v3, axis=0)


def _decode_kernel(pt_ref, q_ref, *refs, hd, lam_init, pages):
    del pt_ref
    kc_refs, vc_refs = refs[:pages], refs[pages:2 * pages]
    (kn_ref, vn_ref, ones_ref, lq1_ref, lk1_ref, lq2_ref, lk2_ref, g_ref,
     o_ref, m_ref, l_ref, acc_ref) = refs[2 * pages:]
    p_idx = pl.program_id(1)

    @pl.when(p_idx == 0)
    def _():
        m_ref[...] = jnp.full(m_ref.shape, NEG_BIG, F32)
        l_ref[...] = jnp.zeros(l_ref.shape, F32)
        acc_ref[...] = jnp.zeros(acc_ref.shape, F32)

    q = q_ref[...]
    ones = ones_ref[...]
    k3 = jnp.concatenate([r[...] for r in kc_refs], axis=0)
    v3 = jnp.concatenate([r[...] for r in vc_refs], axis=0)
    _decode_update(k3, v3, q, ones, m_ref, l_ref, acc_ref, hd)

    @pl.when(p_idx == pl.num_programs(1) - 1)
    def _():
        _decode_update(kn_ref[...][None], vn_ref[...][None], q, ones, m_ref, l_ref, acc_ref, hd)
        lam = _lambda(lq1_ref, lk1_ref, lq2_ref, lk2_ref, lam_init)
        l = l_ref[...]
        l1 = jnp.concatenate([l[:, :hd], l[:, :hd]], axis=-1)
        l2 = jnp.concatenate([l[:, hd:], l[:, hd:]], axis=-1)
        o = acc_ref[0] / l1 - lam * (acc_ref[1] / l2)
        o_ref[...] = _subln(o, g_ref[...], lam_init)


def _decode_attn(page_table, q, cache_k, cache_v, k_new, v_new, lq1, lk1, lq2, lk2, subln_g, *, layer, lam_init,
                 pages):
    b, n_heads, hw = q.shape
    hd = hw // 2
    n_pages = page_table.shape[1]
    assert n_pages % pages == 0
    page = cache_k.shape[2]
    half = lax.broadcasted_iota(jnp.int32, (hw, hw), 0) // hd == lax.broadcasted_iota(jnp.int32, (hw, hw), 1) // hd
    ones = half.astype(BF16)
    per_seq = pl.BlockSpec((None, n_heads, hw), lambda i, p, pt: (i, 0, 0))
    paged = [pl.BlockSpec((None, None, page, n_heads, hw),
                          lambda i, p, pt, k=k: (layer, pt[i, p * pages + k], 0, 0, 0)) for k in range(pages)]
    vec = pl.BlockSpec((1, hd), lambda i, p, pt: (0, 0))
    grid_spec = pltpu.PrefetchScalarGridSpec(
        num_scalar_prefetch=1,
        grid=(b, n_pages // pages),
        in_specs=[per_seq] + paged + paged + [per_seq, per_seq,
                                               pl.BlockSpec((hw, hw), lambda i, p, pt: (0, 0)),
                                               vec, vec, vec, vec,
                                               pl.BlockSpec((1, hw), lambda i, p, pt: (0, 0))],
        out_specs=per_seq,
        scratch_shapes=[pltpu.VMEM((n_heads, hw), F32), pltpu.VMEM((n_heads, hw), F32),
                        pltpu.VMEM((2, n_heads, hw), F32)],
    )
    return pl.pallas_call(
        functools.partial(_decode_kernel, hd=hd, lam_init=lam_init, pages=pages),
        grid_spec=grid_spec,
        out_shape=jax.ShapeDtypeStruct((b, n_heads, hw), F32),
        compiler_params=_cparams(("parallel", "arbitrary")),
        name="decode_attn",
    )(page_table, q, *([cache_k] * pages), *([cache_v] * pages), k_new, v_new, ones, lq1, lk1, lq2, lk2, subln_g)


def _ssm_prep_kernel(ar_ref, ai_ref, ldt_ref, br_ref, bi_ref, abr_ref, abi_ref, bbr_ref, bbi_ref):
    ar = ar_ref[...]
    ai = ai_ref[...]
    dt = jnp.exp(ldt_ref[...])
    mag = jnp.exp(ar * dt)
    ang = ai * dt
    abr = mag * jnp.cos(ang)
    abi = mag * jnp.sin(ang)
    den = ar * ar + ai * ai
    nr = abr - 1.0
    ni = abi
    f_re = (nr * ar + ni * ai) / den
    f_im = (ni * ar - nr * ai) / den
    br = br_ref[...]
    bi = bi_ref[...]
    abr_ref[...] = abr
    abi_ref[...] = abi
    bbr_ref[...] = f_re * br - f_im * bi
    bbi_ref[...] = f_re * bi + f_im * br


def _ssm_prep(a_re, a_im, log_dt, b_re, b_im):
    g, p = a_re.shape
    j = b_re.shape[2]
    s = g * p
    row = lambda x: x.reshape(1, s)
    bt = lambda x: x.reshape(s, j).T
    return pl.pallas_call(
        _ssm_prep_kernel,
        out_shape=[jax.ShapeDtypeStruct((1, s), F32), jax.ShapeDtypeStruct((1, s), F32),
                   jax.ShapeDtypeStruct((j, s), F32), jax.ShapeDtypeStruct((j, s), F32)],
        name="ssm_prep",
    )(row(a_re), row(a_im), row(jnp.broadcast_to(log_dt[:, None], (g, p))), bt(b_re), bt(b_im))


def _block_diag_in(bb, n_groups, p):
    j = bb.shape[0]
    tg = SSM_TILE_GROUPS
    nt = n_groups // tg
    x = bb.reshape(j, nt, tg, p).transpose(1, 0, 2, 3)
    eye = jnp.eye(tg, dtype=bb.dtype)
    x = eye[None, :, None, :, None] * x[:, None]
    return x.reshape(nt, tg * j, tg * p)


def _block_diag_out(c, n_groups):
    _, j, p = c.shape
    tg = SSM_TILE_GROUPS
    nt = n_groups // tg
    x = c.reshape(nt, tg, j, p).transpose(0, 1, 3, 2)
    eye = jnp.eye(tg, dtype=c.dtype)
    x = x[:, :, :, None, :] * eye[None, :, None, :, None]
    return x.reshape(nt, tg * p, tg * j)


def _ssm_in_proj(u, bre_ref, bim_ref, bur_ref, bui_ref):
    nt, kin, nout = bre_ref.shape
    for t in range(nt):
        ut = u[:, t * kin:(t + 1) * kin].astype(BF16)
        bur_ref[:, t * nout:(t + 1) * nout] = jnp.dot(ut, bre_ref[t], preferred_element_type=F32)
        bui_ref[:, t * nout:(t + 1) * nout] = jnp.dot(ut, bim_ref[t], preferred_element_type=F32)


def _ssm_out(xr_ref, xi_ref, u, cre_ref, cim_ref, d_ref, wglu_ref, o_ref):
    nt, kin, nout = cre_ref.shape
    parts = []
    for t in range(nt):
        xr = xr_ref[:, t * kin:(t + 1) * kin].astype(BF16)
        xi = xi_ref[:, t * kin:(t + 1) * kin].astype(BF16)
        parts.append(jnp.dot(xr, cre_ref[t], preferred_element_type=F32)
                     - jnp.dot(xi, cim_ref[t], preferred_element_type=F32))
    y = jnp.concatenate(parts, axis=-1) + d_ref[...] * u
    y = _gelu_tanh(y)
    y = y * _sigmoid(jnp.dot(y.astype(BF16), wglu_ref[...], preferred_element_type=F32))
    o_ref[...] = y.astype(o_ref.dtype)


def _ssm_seq_kernel(u_ref, abr_ref, abi_ref, bre_ref, bim_ref, cre_ref, cim_ref, d_ref, wglu_ref,
                    o_ref, sr_ref, si_ref, xr_ref, xi_ref, cr_ref, ci_ref, *, lane_chunk):
    c = pl.program_id(1)
    t_len = u_ref.shape[0]
    n_state = abr_ref.shape[1]

    @pl.when(c == 0)
    def _():
        cr_ref[...] = jnp.zeros(cr_ref.shape, F32)
        ci_ref[...] = jnp.zeros(ci_ref.shape, F32)

    u = u_ref[...]
    _ssm_in_proj(u, bre_ref, bim_ref, xr_ref, xi_ref)

    for lc in range(n_state // lane_chunk):
        sl = slice(lc * lane_chunk, (lc + 1) * lane_chunk)
        ar = abr_ref[:, sl]
        ai = abi_ref[:, sl]

        def step(t, carry, sl=sl, ar=ar, ai=ai):
            xr, xi = carry
            nr = ar * xr - ai * xi + xr_ref[pl.ds(t, 1), sl]
            ni = ar * xi + ai * xr + xi_ref[pl.ds(t, 1), sl]
            xr_ref[pl.ds(t, 1), sl] = nr
            xi_ref[pl.ds(t, 1), sl] = ni
            return nr, ni

        xr, xi = lax.fori_loop(0, t_len, step, (cr_ref[:, sl], ci_ref[:, sl]), unroll=8)
        cr_ref[:, sl] = xr
        ci_ref[:, sl] = xi

    _ssm_out(xr_ref, xi_ref, u, cre_ref, cim_ref, d_ref, wglu_ref, o_ref)

    @pl.when(c == pl.num_programs(1) - 1)
    def _():
        sr_ref[...] = cr_ref[...]
        si_ref[...] = ci_ref[...]


def _ssm_seq(z3, abr, abi, bre, bim, cre, cim, d, wglu, *, d_ssm, u_col, lc_rows):
    b, t, _ = z3.shape
    n_state = abr.shape[1]
    full2 = lambda a: pl.BlockSpec(a.shape, lambda i, c: (0, 0))
    full3 = lambda a: pl.BlockSpec(a.shape, lambda i, c: (0, 0, 0))
    return pl.pallas_call(
        functools.partial(_ssm_seq_kernel, lane_chunk=1024),
        grid=(b, t // lc_rows),
        in_specs=[pl.BlockSpec((None, lc_rows, d_ssm), lambda i, c: (i, c, u_col)),
                  full2(abr), full2(abi), full3(bre), full3(bim), full3(cre), full3(cim), full2(d), full2(wglu)],
        out_specs=[pl.BlockSpec((None, lc_rows, d_ssm), lambda i, c: (i, c, 0)),
                   pl.BlockSpec((None, 1, n_state), lambda i, c: (i, 0, 0)),
                   pl.BlockSpec((None, 1, n_state), lambda i, c: (i, 0, 0))],
        out_shape=[jax.ShapeDtypeStruct((b, t, d_ssm), BF16),
                   jax.ShapeDtypeStruct((b, 1, n_state), F32),
                   jax.ShapeDtypeStruct((b, 1, n_state), F32)],
        scratch_shapes=[pltpu.VMEM((lc_rows, n_state), F32), pltpu.VMEM((lc_rows, n_state), F32),
                        pltpu.VMEM((1, n_state), F32), pltpu.VMEM((1, n_state), F32)],
        compiler_params=_cparams(("parallel", "arbitrary")),
        name="ssm_seq",
    )(z3, abr, abi, bre, bim, cre, cim, d, wglu)


def _ssm_step_kernel(u_ref, hr_ref, hi_ref, abr_ref, abi_ref, bre_ref, bim_ref, cre_ref, cim_ref, d_ref, wglu32_ref,
                     o_ref, sr_ref, si_ref, wglu_ref):
    wglu_ref[...] = wglu32_ref[...].astype(BF16)
    u = u_ref[...]
    _ssm_in_proj(u, bre_ref, bim_ref, sr_ref, si_ref)
    ar = abr_ref[...]
    ai = abi_ref[...]
    hr = hr_ref[...]
    hi = hi_ref[...]
    sr_ref[...] = ar * hr - ai * hi + sr_ref[...]
    si_ref[...] = ar * hi + ai * hr + si_ref[...]
    _ssm_out(sr_ref, si_ref, u, cre_ref, cim_ref, d_ref, wglu_ref, o_ref)


def _ssm_step(z, h_re, h_im, abr, abi, bre, bim, cre, cim, d, wglu, *, layer, d_ssm, u_col):
    b = z.shape[0]
    n_state = abr.shape[1]
    full2 = lambda a: pl.BlockSpec(a.shape, lambda i: (0, 0))
    full3 = lambda a: pl.BlockSpec(a.shape, lambda i: (0, 0, 0))
    return pl.pallas_call(
        _ssm_step_kernel,
        grid=(1,),
        in_specs=[pl.BlockSpec((b, d_ssm), lambda i: (0, u_col)),
                  full2(h_re), full2(h_im), full2(abr), full2(abi),
                  full3(bre), full3(bim), full3(cre), full3(cim), full2(d),
                  pl.BlockSpec((None, d_ssm, d_ssm), lambda i: (layer, 0, 0))],
        out_specs=[pl.BlockSpec((b, d_ssm), lambda i: (0, 0)),
                   pl.BlockSpec((b, n_state), lambda i: (0, 0)),
                   pl.BlockSpec((b, n_state), lambda i: (0, 0)),
                   pl.BlockSpec((d_ssm, d_ssm), lambda i: (0, 0))],
        out_shape=[jax.ShapeDtypeStruct((b, d_ssm), BF16),
                   jax.ShapeDtypeStruct((b, n_state), F32),
                   jax.ShapeDtypeStruct((b, n_state), F32),
                   jax.ShapeDtypeStruct((d_ssm, d_ssm), BF16)],
        compiler_params=_cparams(("arbitrary",)),
        name="ssm_step",
    )(z, h_re, h_im, abr, abi, bre, bim, cre, cim, d, wglu)


def _conv_post(yc_ref, cb_ref, lg_ref, lb_ref, hc_ref):
    rows = yc_ref.shape[0]
    chunk = _row_chunk(rows)
    bias = cb_ref[...]
    lg = lg_ref[...]
    lb = lb_ref[...]

    def body(c, carry):
        r0 = pl.multiple_of(c * chunk, chunk)
        y = yc_ref[pl.ds(r0, chunk), :] + bias
        mu = jnp.mean(y, axis=-1, keepdims=True)
        yc = y - mu
        var = jnp.mean(yc * yc, axis=-1, keepdims=True)
        hc_ref[pl.ds(r0, chunk), :] = _silu((yc * lax.rsqrt(var + EPS)) * lg + lb).astype(BF16)
        return carry

    lax.fori_loop(0, rows // chunk, body, 0, unroll=4)


def _merge_math(hc, att, ys, ga, gb, gc, wc, wa, ws):
    a = jnp.dot(hc, wc, preferred_element_type=F32)
    b = jnp.dot(att.astype(BF16), wa, preferred_element_type=F32)
    c = jnp.dot(ys, ws, preferred_element_type=F32)
    return (_sigmoid(ga) * a + _sigmoid(gb) * b + _sigmoid(gc) * c).astype(BF16)


def _merge_kernel(yc_ref, att_ref, ys_ref, ga_ref, gb_ref, gc_ref, cb_ref, lg_ref, lb_ref,
                  wc_ref, wa_ref, ws_ref, o_ref, hc_ref):
    @pl.when(pl.program_id(1) == 0)
    def _():
        _conv_post(yc_ref, cb_ref, lg_ref, lb_ref, hc_ref)

    o_ref[...] = _merge_math(hc_ref[...], att_ref[...], ys_ref[...], ga_ref[...], gb_ref[...], gc_ref[...],
                             wc_ref[...], wa_ref[...], ws_ref[...])


def _merge_cast_kernel(yc_ref, att_ref, ys_ref, ga_ref, gb_ref, gc_ref, cb_ref, lg_ref, lb_ref,
                       wc_ref, wa_ref, ws_ref, o_ref, wcb_ref, wab_ref, wsb_ref, hc_ref):
    @pl.when(pl.program_id(0) == 0)
    def _():
        _conv_post(yc_ref, cb_ref, lg_ref, lb_ref, hc_ref)

    wcb_ref[...] = wc_ref[...].astype(BF16)
    wab_ref[...] = wa_ref[...].astype(BF16)
    wsb_ref[...] = ws_ref[...].astype(BF16)
    o_ref[...] = _merge_math(hc_ref[...], att_ref[...], ys_ref[...], ga_ref[...], gb_ref[...], gc_ref[...],
                             wcb_ref[...], wab_ref[...], wsb_ref[...])


def _merge_cast(yconv, att, ys, z, conv_b, ln_g, ln_b, w_conv, w_attn, w_ssm, *, layer, gate_col, bn):
    m, d_conv = yconv.shape
    d_attn = att.shape[1]
    d_ssm = ys.shape[1]
    d = w_conv.shape[2]
    assert d % bn == 0 and gate_col % bn == 0
    g0 = gate_col // bn
    nj = d // bn
    vec = pl.BlockSpec((1, d_conv), lambda j: (0, 0))
    wspec = lambda k: pl.BlockSpec((None, k, bn), lambda j: (layer, 0, j))
    ospec = lambda k: pl.BlockSpec((k, bn), lambda j: (0, j))
    return pl.pallas_call(
        _merge_cast_kernel,
        grid=(nj,),
        in_specs=[
            pl.BlockSpec((m, d_conv), lambda j: (0, 0)),
            pl.BlockSpec((m, d_attn), lambda j: (0, 0)),
            pl.BlockSpec((m, d_ssm), lambda j: (0, 0)),
            pl.BlockSpec((m, bn), lambda j: (0, g0 + j)),
            pl.BlockSpec((m, bn), lambda j: (0, g0 + nj + j)),
            pl.BlockSpec((m, bn), lambda j: (0, g0 + 2 * nj + j)),
            vec, vec, vec,
            wspec(d_conv), wspec(d_attn), wspec(d_ssm),
        ],
        out_specs=[pl.BlockSpec((m, bn), lambda j: (0, j)), ospec(d_conv), ospec(d_attn), ospec(d_ssm)],
        out_shape=[jax.ShapeDtypeStruct((m, d), BF16), jax.ShapeDtypeStruct((d_conv, d), BF16),
                   jax.ShapeDtypeStruct((d_attn, d), BF16), jax.ShapeDtypeStruct((d_ssm, d), BF16)],
        scratch_shapes=[pltpu.VMEM((m, d_conv), BF16)],
        compiler_params=_cparams(("arbitrary",)),
        name="merge_cast",
    )(yconv, att, ys, z, z, z, conv_b, ln_g, ln_b, w_conv, w_attn, w_ssm)


def _merge(yconv, att, ys, z, conv_b, ln_g, ln_b, w_conv, w_attn, w_ssm, *, gate_col, bm, bn):
    m, d_conv = yconv.shape
    d_attn = att.shape[1]
    d_ssm = ys.shape[1]
    d = w_conv.shape[1]
    assert m % bm == 0 and d % bn == 0 and gate_col % bn == 0
    g0 = gate_col // bn
    nj = d // bn
    vec = pl.BlockSpec((1, d_conv), lambda i, j: (0, 0))
    return pl.pallas_call(
        _merge_kernel,
        grid=(m // bm, nj),
        in_specs=[
            pl.BlockSpec((bm, d_conv), lambda i, j: (i, 0)),
            pl.BlockSpec((bm, d_attn), lambda i, j: (i, 0)),
            pl.BlockSpec((bm, d_ssm), lambda i, j: (i, 0)),
            pl.BlockSpec((bm, bn), lambda i, j: (i, g0 + j)),
            pl.BlockSpec((bm, bn), lambda i, j: (i, g0 + nj + j)),
            pl.BlockSpec((bm, bn), lambda i, j: (i, g0 + 2 * nj + j)),
            vec, vec, vec,
            pl.BlockSpec((d_conv, bn), lambda i, j: (0, j)),
            pl.BlockSpec((d_attn, bn), lambda i, j: (0, j)),
            pl.BlockSpec((d_ssm, bn), lambda i, j: (0, j)),
        ],
        out_specs=pl.BlockSpec((bm, bn), lambda i, j: (i, j)),
        out_shape=jax.ShapeDtypeStruct((m, d), BF16),
        scratch_shapes=[pltpu.VMEM((bm, d_conv), BF16)],
        compiler_params=_cparams(("parallel", "arbitrary")),
        name="merge",
    )(yconv, att, ys, z, z, z, conv_b, ln_g, ln_b, w_conv, w_attn, w_ssm)


def _resproj_kernel(a_ref, w_ref, x_ref, o_ref):
    o_ref[...] = x_ref[...] + jnp.dot(a_ref[...], w_ref[...], preferred_element_type=F32)


def _resproj(a, w, x, *, bm, bn):
    m, k = a.shape
    n = w.shape[1]
    assert m % bm == 0 and n % bn == 0
    return pl.pallas_call(
        _resproj_kernel,
        grid=(m // bm, n // bn),
        in_specs=[
            pl.BlockSpec((bm, k), lambda i, j: (i, 0)),
            pl.BlockSpec((k, bn), lambda i, j: (0, j)),
            pl.BlockSpec((bm, bn), lambda i, j: (i, j)),
        ],
        out_specs=pl.BlockSpec((bm, bn), lambda i, j: (i, j)),
        out_shape=jax.ShapeDtypeStruct((m, n), F32),
        compiler_params=_cparams(("parallel", "arbitrary")),
        name="resproj",
    )(a, w, x)


def _resproj_cast_kernel(a_ref, w_ref, x_ref, o_ref, wb_ref):
    wb_ref[...] = w_ref[...].astype(BF16)
    o_ref[...] = x_ref[...] + jnp.dot(a_ref[...], wb_ref[...], preferred_element_type=F32)


def _resproj_cast(a, w, x, *, layer, bn):
    m, k = a.shape
    n = w.shape[2]
    assert n % bn == 0
    return pl.pallas_call(
        _resproj_cast_kernel,
        grid=(n // bn,),
        in_specs=[
            pl.BlockSpec((m, k), lambda j: (0, 0)),
            pl.BlockSpec((None, k, bn), lambda j: (layer, 0, j)),
            pl.BlockSpec((m, bn), lambda j: (0, j)),
        ],
        out_specs=[pl.BlockSpec((m, bn), lambda j: (0, j)), pl.BlockSpec((k, bn), lambda j: (0, j))],
        out_shape=[jax.ShapeDtypeStruct((m, n), F32), jax.ShapeDtypeStruct((k, n), BF16)],
        compiler_params=_cparams(("arbitrary",)),
        name="resproj_cast",
    )(a, w, x)


def _final_norm_kernel(x_ref, g_ref, o_ref):
    _rms_rows(x_ref, g_ref, o_ref)


def _final_norm(x, g, *, bm):
    m, d = x.shape
    assert m % bm == 0
    return pl.pallas_call(
        _final_norm_kernel,
        grid=(m // bm,),
        in_specs=[pl.BlockSpec((bm, d), lambda i: (i, 0)), pl.BlockSpec((1, d), lambda i: (0, 0))],
        out_specs=pl.BlockSpec((bm, d), lambda i: (i, 0)),
        out_shape=jax.ShapeDtypeStruct((m, d), F32),
        compiler_params=_cparams(("parallel",)),
        name="final_norm",
    )(x, g)


def kernel(x_prompt, x_sample, cache_k, cache_v, page_table, state_conv, state_ssm_re, state_ssm_im, meta_tokens, norm_ffn1, ffn1_gate, ffn1_up, ffn1_down, norm_mix, w_in, conv_w, conv_b, conv_ln_g, conv_ln_b, conv_proj, lambda_q1, lambda_k1, lambda_q2, lambda_k2, attn_subln_g, attn_proj, ssm_a_re, ssm_a_im, ssm_log_dt, ssm_b_re, ssm_b_im, ssm_c_re, ssm_c_im, ssm_d, ssm_glu, ssm_proj, w_out, norm_ffn2, ffn2_gate, ffn2_up, ffn2_down, norm_final):
    bp, seq, d = x_prompt.shape
    db = x_sample.shape[0]
    depth = w_in.shape[0]
    d_conv = conv_w.shape[2]
    hd = lambda_q1.shape[1]
    hw = 2 * hd
    n_heads = cache_k.shape[3]
    d_attn = n_heads * hw
    n_groups, n_state_g = ssm_a_re.shape[1], ssm_a_re.shape[2]
    d_ssm = ssm_d.shape[1]
    n_state = n_groups * n_state_g
    t = seq + N_META
    mp = bp * t
    q_off = 2 * d_conv
    k_off = q_off + d_attn
    v_off = k_off + d_attn
    u_off = v_off + d_attn
    gate_off = u_off + d_ssm

    bm_p = t // 3
    row2 = lambda v: v.reshape(1, -1)

    xp = jnp.concatenate([jnp.broadcast_to(meta_tokens[None], (bp, N_META, d)), x_prompt], axis=1).reshape(mp, d)
    xs = x_sample.reshape(db, d)
    zero_conv = jnp.zeros((bp, CONV_PAD, d_conv), F32)

    outs_p = {k: [] for k in ("k", "v", "conv", "sre", "sim")}
    outs_s = {k: [] for k in ("k", "v", "conv", "sre", "sim")}
    for l in range(depth):
        lam_init = 0.8 - 0.6 * math.exp(-0.3 * l)
        lq1, lk1, lq2, lk2 = row2(lambda_q1[l]), row2(lambda_k1[l]), row2(lambda_q2[l]), row2(lambda_k2[l])
        sub_g = row2(attn_subln_g[l])
        cb, lng, lnb = row2(conv_b[l]), row2(conv_ln_g[l]), row2(conv_ln_b[l])

        abr, abi, bbr, bbi = _ssm_prep(ssm_a_re[l], ssm_a_im[l], ssm_log_dt[l], ssm_b_re[l], ssm_b_im[l])
        bre = _block_diag_in(bbr, n_groups, n_state_g).astype(BF16)
        bim = _block_diag_in(bbi, n_groups, n_state_g).astype(BF16)
        cre = _block_diag_out(ssm_c_re[l], n_groups).astype(BF16)
        cim = _block_diag_out(ssm_c_im[l], n_groups).astype(BF16)
        dvec = row2(ssm_d[l])

        xs, wg1, wu1, wd1 = _ffn_cast(xs, row2(norm_ffn1[l]), ffn1_gate, ffn1_up, ffn1_down, layer=l, bf=FFN_CAST_BF)
        xp = _ffn(xp, row2(norm_ffn1[l]), wg1, wu1, wd1, bm=bm_p, bf=FFN_BF)

        zs, win = _inproj_cast(xs, row2(norm_mix[l]), w_in, layer=l, bn=512)
        zp = _inproj(xp, row2(norm_mix[l]), win, bm=bm_p, bn=1024)
        zp3 = zp.reshape(bp, t, -1)

        yconv_s, conv_s = _conv_step(zs, state_conv[l], conv_w[l], d_conv=d_conv)
        q_s = zs[:, q_off:k_off].reshape(db, n_heads, hw)
        k_s = zs[:, k_off:v_off].reshape(db, n_heads, hw)
        v_s = zs[:, v_off:u_off].reshape(db, n_heads, hw)
        att_s = _decode_attn(page_table, q_s, cache_k, cache_v, k_s, v_s, lq1, lk1, lq2, lk2, sub_g,
                             layer=l, lam_init=lam_init, pages=DECODE_PAGES)
        ys_s, sre_s, sim_s, wglu = _ssm_step(zs, state_ssm_re[l].reshape(db, n_state),
                                             state_ssm_im[l].reshape(db, n_state), abr, abi, bre, bim, cre, cim, dvec,
                                             ssm_glu, layer=l, d_ssm=d_ssm, u_col=u_off // d_ssm)

        yconv_p, conv_p = _conv_seq(zp3, zero_conv, conv_w[l], d_conv=d_conv, cb=256, rows=48)
        att_p, k_p, v_p = _flash(zp3, lq1, lk1, lq2, lk2, sub_g, n_heads=n_heads, hd=hd, q_col=q_off // hw,
                                 k_col=k_off // hw, v_col=v_off // hw, tq=t // 3, tk=t // 3, lam_init=lam_init)
        outs_p["k"].append(k_p.reshape(bp, t, n_heads, hw))
        outs_p["v"].append(v_p.reshape(bp, t, n_heads, hw))
        ys_p, sre_p, sim_p = _ssm_seq(zp3, abr, abi, bre, bim, cre, cim, dvec, wglu,
                                      d_ssm=d_ssm, u_col=u_off // d_ssm, lc_rows=t // 3)

        merged_s, wconv, wattn, wssm = _merge_cast(yconv_s, att_s.reshape(db, d_attn), ys_s, zs, cb, lng, lnb,
                                                   conv_proj, attn_proj, ssm_proj, layer=l, gate_col=gate_off, bn=512)
        merged_p = _merge(yconv_p.reshape(mp, d_conv), att_p.reshape(mp, d_attn), ys_p.reshape(mp, d_ssm), zp,
                          cb, lng, lnb, wconv, wattn, wssm, gate_col=gate_off, bm=bm_p, bn=512)
        xs, wout = _resproj_cast(merged_s, w_out, xs, layer=l, bn=512)
        xp = _resproj(merged_p, wout, xp, bm=bm_p, bn=1024)

        xs, wg2, wu2, wd2 = _ffn_cast(xs, row2(norm_ffn2[l]), ffn2_gate, ffn2_up, ffn2_down, layer=l, bf=FFN_CAST_BF)
        xp = _ffn(xp, row2(norm_ffn2[l]), wg2, wu2, wd2, bm=bm_p, bf=FFN_BF)

        outs_p["conv"].append(conv_p)
        outs_p["sre"].append(sre_p.reshape(bp, n_groups, n_state_g))
        outs_p["sim"].append(sim_p.reshape(bp, n_groups, n_state_g))
        outs_s["k"].append(k_s.reshape(db, 1, n_heads, hw))
        outs_s["v"].append(v_s.reshape(db, 1, n_heads, hw))
        outs_s["conv"].append(conv_s)
        outs_s["sre"].append(sre_s.reshape(db, n_groups, n_state_g))
        outs_s["sim"].append(sim_s.reshape(db, n_groups, n_state_g))

    y_prompt = _final_norm(xp, row2(norm_final), bm=bm_p).reshape(bp, t, d)[:, N_META:]
    y_sample = _final_norm(xs, row2(norm_final), bm=db).reshape(db, 1, d)
    st = jnp.stack
    return (y_prompt, y_sample, st(outs_p["k"]), st(outs_p["v"]), st(outs_p["conv"]), st(outs_p["sre"]),
            st(outs_p["sim"]), st(outs_s["k"]), st(outs_s["v"]), st(outs_s["conv"]), st(outs_s["sre"]),
            st(outs_s["sim"]))
```

```python
import functools
import math

import jax
import jax.numpy as jnp
from jax import lax
from jax.experimental import pallas as pl
from jax.experimental.pallas import tpu as pltpu

F32 = jnp.float32
BF16 = jnp.bfloat16

EPS = 1e-6
NEG_BIG = -1e30
LOG2_E = 1.4426950408889634
CONV_WIDTH = 31
CONV_PAD = 32
N_META = 16
SSM_TILE_GROUPS = 8
FFN_BF = 256
FFN_CAST_BF = 256
DECODE_PAGES = 8
VMEM_LIMIT = 60 * 1024 * 1024


def _cparams(sem):
    return pltpu.CompilerParams(dimension_semantics=sem, vmem_limit_bytes=VMEM_LIMIT)


def _sigmoid(x):
    return 1.0 / (1.0 + jnp.exp(-x))


def _silu(x):
    return x * _sigmoid(x)


def _gelu_tanh(x):
    return 0.5 * x * (1.0 + jnp.tanh(math.sqrt(2.0 / math.pi) * (x + 0.044715 * (x * x * x))))


def _row_chunk(rows):
    for c in (16, 8):
        if rows % c == 0:
            return c
    raise ValueError(f"row count {rows} must be a multiple of 8")


def _rms_rows(x_ref, g_ref, h_ref):
    rows = x_ref.shape[0]
    assert rows % 8 == 0
    chunk = min(16, rows)
    g = g_ref[...]

    def norm(r0, n):
        x = x_ref[pl.ds(r0, n), :]
        ms = jnp.mean(x * x, axis=-1, keepdims=True)
        h_ref[pl.ds(r0, n), :] = ((x * lax.rsqrt(ms + EPS)) * g).astype(h_ref.dtype)

    def body(c, carry):
        norm(pl.multiple_of(c * chunk, chunk), chunk)
        return carry

    n_main = rows // chunk
    lax.fori_loop(0, n_main, body, 0, unroll=2)
    if rows > n_main * chunk:
        norm(n_main * chunk, rows - n_main * chunk)


def _row_splits(rows, parts):
    n_units = max(rows // 16, 1)
    parts = min(parts, n_units)
    base, extra = divmod(n_units, parts)
    out, r0 = [], 0
    for k in range(parts):
        n = (base + (1 if k < extra else 0)) * 16
        if k == parts - 1:
            n = rows - r0
        out.append((r0, n))
        r0 += n
    return tuple(out)


def _ffn_math(h, wg, wu, wd):
    gate = jnp.dot(h, wg, preferred_element_type=F32)
    up = jnp.dot(h, wu, preferred_element_type=F32)
    a = (0.5 * (_silu(gate) * up)).astype(BF16)
    return jnp.dot(a, wd, preferred_element_type=F32)


def _ffn_kernel(x_hbm, g_ref, wg_ref, wu_ref, wd_ref, o_ref, h_ref, sem, *, bm, splits):
    i = pl.program_id(0)
    f = pl.program_id(1)

    @pl.when(f == 0)
    def _():
        copies = [pltpu.make_async_copy(x_hbm.at[pl.ds(i * bm + r0, n), :], o_ref.at[pl.ds(r0, n), :], sem.at[k])
                  for k, (r0, n) in enumerate(splits)]
        for cp in copies:
            cp.start()
        for cp, (r0, n) in zip(copies, splits):
            cp.wait()
            _rms_rows(o_ref.at[pl.ds(r0, n), :], g_ref, h_ref.at[pl.ds(r0, n), :])

    o_ref[...] += _ffn_math(h_ref[...], wg_ref[...], wu_ref[...], wd_ref[...])


def _ffn(x, g, wg, wu, wd, *, bm, bf):
    m, d = x.shape
    f_dim = wg.shape[1]
    assert m % bm == 0 and f_dim % bf == 0
    splits = _row_splits(bm, 4)
    return pl.pallas_call(
        functools.partial(_ffn_kernel, bm=bm, splits=splits),
        grid=(m // bm, f_dim // bf),
        in_specs=[
            pl.BlockSpec(memory_space=pl.ANY),
            pl.BlockSpec((1, d), lambda i, f: (0, 0)),
            pl.BlockSpec((d, bf), lambda i, f: (0, f)),
            pl.BlockSpec((d, bf), lambda i, f: (0, f)),
            pl.BlockSpec((bf, d), lambda i, f: (f, 0)),
        ],
        out_specs=pl.BlockSpec((bm, d), lambda i, f: (i, 0)),
        out_shape=jax.ShapeDtypeStruct((m, d), F32),
        scratch_shapes=[pltpu.VMEM((bm, d), BF16), pltpu.SemaphoreType.DMA((len(splits),))],
        compiler_params=_cparams(("parallel", "arbitrary")),
        name="ffn",
    )(x, g, wg, wu, wd)


def _ffn_cast_kernel(x_ref, g_ref, wg_ref, wu_ref, wd_ref, o_ref, wgb_ref, wub_ref, wdb_ref, h_ref):
    @pl.when(pl.program_id(0) == 0)
    def _():
        _rms_rows(x_ref, g_ref, h_ref)
        o_ref[...] = x_ref[...]

    wgb_ref[...] = wg_ref[...].astype(BF16)
    wub_ref[...] = wu_ref[...].astype(BF16)
    wdb_ref[...] = wd_ref[...].astype(BF16)
    o_ref[...] += _ffn_math(h_ref[...], wgb_ref[...], wub_ref[...], wdb_ref[...])


def _ffn_cast(x, g, wg, wu, wd, *, layer, bf):
    m, d = x.shape
    f_dim = wg.shape[2]
    assert f_dim % bf == 0
    return pl.pallas_call(
        _ffn_cast_kernel,
        grid=(f_dim // bf,),
        in_specs=[
            pl.BlockSpec((m, d), lambda f: (0, 0)),
            pl.BlockSpec((1, d), lambda f: (0, 0)),
            pl.BlockSpec((None, d, bf), lambda f: (layer, 0, f)),
            pl.BlockSpec((None, d, bf), lambda f: (layer, 0, f)),
            pl.BlockSpec((None, bf, d), lambda f: (layer, f, 0)),
        ],
        out_specs=[
            pl.BlockSpec((m, d), lambda f: (0, 0)),
            pl.BlockSpec((d, bf), lambda f: (0, f)),
            pl.BlockSpec((d, bf), lambda f: (0, f)),
            pl.BlockSpec((bf, d), lambda f: (f, 0)),
        ],
        out_shape=[
            jax.ShapeDtypeStruct((m, d), F32),
            jax.ShapeDtypeStruct((d, f_dim), BF16),
            jax.ShapeDtypeStruct((d, f_dim), BF16),
            jax.ShapeDtypeStruct((f_dim, d), BF16),
        ],
        scratch_shapes=[pltpu.VMEM((m, d), BF16)],
        compiler_params=_cparams(("arbitrary",)),
        name="ffn_cast",
    )(x, g, wg, wu, wd)


def _inproj_kernel(x_ref, g_ref, w_ref, o_ref, h_ref):
    @pl.when(pl.program_id(1) == 0)
    def _():
        _rms_rows(x_ref, g_ref, h_ref)

    o_ref[...] = jnp.dot(h_ref[...], w_ref[...], preferred_element_type=F32)


def _inproj(x, g, w, *, bm, bn):
    m, d = x.shape
    n = w.shape[1]
    assert m % bm == 0 and n % bn == 0
    return pl.pallas_call(
        _inproj_kernel,
        grid=(m // bm, n // bn),
        in_specs=[
            pl.BlockSpec((bm, d), lambda i, j: (i, 0)),
            pl.BlockSpec((1, d), lambda i, j: (0, 0)),
            pl.BlockSpec((d, bn), lambda i, j: (0, j)),
        ],
        out_specs=pl.BlockSpec((bm, bn), lambda i, j: (i, j)),
        out_shape=jax.ShapeDtypeStruct((m, n), F32),
        scratch_shapes=[pltpu.VMEM((bm, d), BF16)],
        compiler_params=_cparams(("parallel", "arbitrary")),
        name="inproj",
    )(x, g, w)


def _inproj_cast_kernel(x_ref, g_ref, w_ref, o_ref, wb_ref, h_ref):
    @pl.when(pl.program_id(0) == 0)
    def _():
        _rms_rows(x_ref, g_ref, h_ref)

    wb_ref[...] = w_ref[...].astype(BF16)
    o_ref[...] = jnp.dot(h_ref[...], wb_ref[...], preferred_element_type=F32)


def _inproj_cast(x, g, w, *, layer, bn):
    m, d = x.shape
    n = w.shape[2]
    assert n % bn == 0
    return pl.pallas_call(
        _inproj_cast_kernel,
        grid=(n // bn,),
        in_specs=[
            pl.BlockSpec((m, d), lambda j: (0, 0)),
            pl.BlockSpec((1, d), lambda j: (0, 0)),
            pl.BlockSpec((None, d, bn), lambda j: (layer, 0, j)),
        ],
        out_specs=[pl.BlockSpec((m, bn), lambda j: (0, j)), pl.BlockSpec((d, bn), lambda j: (0, j))],
        out_shape=[jax.ShapeDtypeStruct((m, n), F32), jax.ShapeDtypeStruct((d, n), BF16)],
        scratch_shapes=[pltpu.VMEM((m, d), BF16)],
        compiler_params=_cparams(("arbitrary",)),
        name="inproj_cast",
    )(x, g, w)


def _conv_seq_kernel(a_ref, gt_ref, st_ref, w_ref, y_ref, ns_ref, ext_ref, *, rows):
    t_len = a_ref.shape[0]
    n_chunks = t_len // rows
    ext_ref[0:CONV_PAD, :] = st_ref[...]

    def glu(c, carry):
        r0 = pl.multiple_of(c * rows, rows)
        ext_ref[pl.ds(CONV_PAD + r0, rows), :] = a_ref[pl.ds(r0, rows), :] * _sigmoid(gt_ref[pl.ds(r0, rows), :])
        return carry

    lax.fori_loop(0, n_chunks, glu, 0)
    w = w_ref[...]
    off = CONV_PAD - (CONV_WIDTH - 1)

    def conv(c, carry):
        r0 = pl.multiple_of(c * rows, rows)
        win = ext_ref[pl.ds(r0, rows + CONV_PAD), :]
        shifted = [win] + [win[r:r + rows + CONV_PAD - 8, :] for r in range(1, 8)]
        acc = None
        for k in range(CONV_WIDTH):
            s = off + k
            tap = w[k:k + 1, :] * shifted[s % 8][8 * (s // 8):8 * (s // 8) + rows, :]
            acc = tap if acc is None else acc + tap
        y_ref[pl.ds(r0, rows), :] = acc
        return carry

    lax.fori_loop(0, n_chunks, conv, 0)
    ns_ref[...] = ext_ref[pl.ds(t_len + off, CONV_WIDTH - 1), :]


def _conv_seq(z3, state_pad, conv_w, *, d_conv, cb, rows):
    b, t, _ = z3.shape
    nj = d_conv // cb
    return pl.pallas_call(
        functools.partial(_conv_seq_kernel, rows=rows),
        grid=(b, nj),
        in_specs=[
            pl.BlockSpec((None, t, cb), lambda i, j: (i, 0, j)),
            pl.BlockSpec((None, t, cb), lambda i, j: (i, 0, nj + j)),
            pl.BlockSpec((None, CONV_PAD, cb), lambda i, j: (i, 0, j)),
            pl.BlockSpec((CONV_WIDTH, cb), lambda i, j: (0, j)),
        ],
        out_specs=[
            pl.BlockSpec((None, t, cb), lambda i, j: (i, 0, j)),
            pl.BlockSpec((None, CONV_WIDTH - 1, cb), lambda i, j: (i, 0, j)),
        ],
        out_shape=[
            jax.ShapeDtypeStruct((b, t, d_conv), F32),
            jax.ShapeDtypeStruct((b, CONV_WIDTH - 1, d_conv), F32),
        ],
        scratch_shapes=[pltpu.VMEM((CONV_PAD + t, cb), F32)],
        compiler_params=_cparams(("parallel", "parallel")),
        name="conv_seq",
    )(z3, z3, state_pad, conv_w)


def _conv_step_kernel(a_ref, gt_ref, st_ref, w_ref, y_ref, ns_ref):
    u = a_ref[...] * _sigmoid(gt_ref[...])
    st = st_ref[...]
    w = w_ref[...]
    y_ref[...] = jnp.sum(st * w[None, 0:CONV_WIDTH - 1, :], axis=1) + u * w[CONV_WIDTH - 1:CONV_WIDTH, :]
    ns_ref[:, 0:CONV_WIDTH - 2, :] = st[:, 1:CONV_WIDTH - 1, :]
    ns_ref[:, CONV_WIDTH - 2:CONV_WIDTH - 1, :] = u[:, None, :]


def _conv_step(z, state, conv_w, *, d_conv):
    b = z.shape[0]
    return pl.pallas_call(
        _conv_step_kernel,
        grid=(1,),
        in_specs=[
            pl.BlockSpec((b, d_conv), lambda i: (0, 0)),
            pl.BlockSpec((b, d_conv), lambda i: (0, 1)),
            pl.BlockSpec((b, CONV_WIDTH - 1, d_conv), lambda i: (0, 0, 0)),
            pl.BlockSpec((CONV_WIDTH, d_conv), lambda i: (0, 0)),
        ],
        out_specs=[
            pl.BlockSpec((b, d_conv), lambda i: (0, 0)),
            pl.BlockSpec((b, CONV_WIDTH - 1, d_conv), lambda i: (0, 0, 0)),
        ],
        out_shape=[
            jax.ShapeDtypeStruct((b, d_conv), F32),
            jax.ShapeDtypeStruct((b, CONV_WIDTH - 1, d_conv), F32),
        ],
        compiler_params=_cparams(("arbitrary",)),
        name="conv_step",
    )(z, z, state, conv_w)


def _lambda(lq1_ref, lk1_ref, lq2_ref, lk2_ref, lam_init):
    s1 = jnp.sum(lq1_ref[...] * lk1_ref[...], axis=-1, keepdims=True)
    s2 = jnp.sum(lq2_ref[...] * lk2_ref[...], axis=-1, keepdims=True)
    return jnp.exp(s1) - jnp.exp(s2) + lam_init


def _subln(o, g, lam_init):
    ms = jnp.mean(o * o, axis=-1, keepdims=True)
    return ((o * lax.rsqrt(ms + EPS)) * g) * (1.0 - lam_init)


def _flash_kernel(q_ref, k_ref, v_ref, lq1_ref, lk1_ref, lq2_ref, lk2_ref, g_ref, o_ref, ko_ref, vo_ref,
                  m_ref, l_ref, acc_ref, *, tq, tk, hd, scale, lam_init):
    qi = pl.program_id(2)

    @pl.when(qi == 0)
    def _():
        ko_ref[...] = k_ref[...]
        vo_ref[...] = v_ref[...]

    q = q_ref[...] * (LOG2_E * scale)
    qs = (q[:, :hd].astype(BF16), q[:, hd:].astype(BF16))
    m_ref[...] = jnp.full(m_ref.shape, NEG_BIG, F32)
    l_ref[...] = jnp.zeros(l_ref.shape, F32)
    acc_ref[...] = jnp.zeros(acc_ref.shape, F32)
    row = qi * tq + lax.broadcasted_iota(jnp.int32, (tq, tk), 0)
    col0 = lax.broadcasted_iota(jnp.int32, (tq, tk), 1)
    n_kv = (qi * tq + tq + tk - 1) // tk
    n_full = (qi * tq + 1) // tk

    def body(j, carry, *, masked):
        k0 = pl.multiple_of(j * tk, tk)
        k = k_ref[pl.ds(k0, tk), :]
        v = v_ref[pl.ds(k0, tk), :].astype(BF16)
        for c in range(2):
            kc = k[:, c * hd:(c + 1) * hd].astype(BF16)
            s = lax.dot_general(qs[c], kc, (((1,), (1,)), ((), ())), preferred_element_type=F32)
            if masked:
                s = jnp.where((col0 + k0) <= row, s, NEG_BIG)
            m_old = m_ref[c]
            m_new = jnp.maximum(m_old, jnp.max(s, axis=-1, keepdims=True))
            alpha = jnp.exp2(m_old - m_new)
            p = jnp.exp2(s - m_new)
            l_ref[c] = alpha * l_ref[c] + jnp.sum(p, axis=-1, keepdims=True)
            acc_ref[c] = alpha * acc_ref[c] + jnp.dot(p.astype(BF16), v, preferred_element_type=F32)
            m_ref[c] = m_new
        return carry

    lax.fori_loop(0, n_full, functools.partial(body, masked=False), 0)
    lax.fori_loop(n_full, n_kv, functools.partial(body, masked=True), 0)
    lam = _lambda(lq1_ref, lk1_ref, lq2_ref, lk2_ref, lam_init)
    o = acc_ref[0] / l_ref[0] - lam * (acc_ref[1] / l_ref[1])
    o_ref[...] = _subln(o, g_ref[...], lam_init).astype(o_ref.dtype)


def _flash(z3, lq1, lk1, lq2, lk2, subln_g, *, n_heads, hd, q_col, k_col, v_col, tq, tk, lam_init):
    b, t, _ = z3.shape
    hw = 2 * hd
    vec = pl.BlockSpec((1, hd), lambda i, h, j: (0, 0))
    slab = pl.BlockSpec((None, t, hw), lambda i, h, j: (i, 0, h))
    kv_shape = jax.ShapeDtypeStruct((b, t, n_heads * hw), F32)
    return pl.pallas_call(
        functools.partial(_flash_kernel, tq=tq, tk=tk, hd=hd, scale=1.0 / math.sqrt(hd), lam_init=lam_init),
        grid=(b, n_heads, t // tq),
        in_specs=[
            pl.BlockSpec((None, tq, hw), lambda i, h, j: (i, j, q_col + h)),
            pl.BlockSpec((None, t, hw), lambda i, h, j: (i, 0, k_col + h)),
            pl.BlockSpec((None, t, hw), lambda i, h, j: (i, 0, v_col + h)),
            vec, vec, vec, vec,
            pl.BlockSpec((1, hw), lambda i, h, j: (0, 0)),
        ],
        out_specs=[pl.BlockSpec((None, tq, hw), lambda i, h, j: (i, j, h)), slab, slab],
        out_shape=[jax.ShapeDtypeStruct((b, t, n_heads * hw), BF16), kv_shape, kv_shape],
        scratch_shapes=[pltpu.VMEM((2, tq, 1), F32), pltpu.VMEM((2, tq, 1), F32), pltpu.VMEM((2, tq, hw), F32)],
        compiler_params=_cparams(("parallel", "parallel", "arbitrary")),
        name="flash",
    )(z3, z3, z3, lq1, lk1, lq2, lk2, subln_g)


def _decode_update(k3, v3, q, ones, m_ref, l_ref, acc_ref, hd):
    n, n_heads, hw = k3.shape
    qs = q * (LOG2_E / math.sqrt(hd))
    prod = (k3 * qs[None]).reshape(n * n_heads, hw).astype(BF16)
    s = jnp.dot(prod, ones, preferred_element_type=F32).reshape(n, n_heads, hw)
    m_old = m_ref[...]
    m_new = jnp.maximum(m_old, jnp.max(s, axis=0))
    alpha = jnp.exp2(m_old - m_new)
    p = jnp.exp2(s - m_new[None])
    l_ref[...] = alpha * l_ref[...] + jnp.sum(p, axis=0)
    m_ref[...] = m_new
    for c in range(2):
        pc = p[:, :, c * hd:(c + 1) * hd]
        pw = jnp.concatenate([pc, pc], axis=-1)
        ac = alpha[:, c * hd:(c + 1) * hd]
        acc_ref[c] = jnp.concatenate([ac, ac], axis=-1) * acc_ref[c] + jnp.sum(pw * v3, axis=0)


def _decode_kernel(pt_ref, q_ref, *refs, hd, lam_init, pages):
    del pt_ref
    kc_refs, vc_refs = refs[:pages], refs[pages:2 * pages]
    (kn_ref, vn_ref, ones_ref, lq1_ref, lk1_ref, lq2_ref, lk2_ref, g_ref,
     o_ref, m_ref, l_ref, acc_ref) = refs[2 * pages:]
    p_idx = pl.program_id(1)

    @pl.when(p_idx == 0)
    def _():
        m_ref[...] = jnp.full(m_ref.shape, NEG_BIG, F32)
        l_ref[...] = jnp.zeros(l_ref.shape, F32)
        acc_ref[...] = jnp.zeros(acc_ref.shape, F32)

    q = q_ref[...]
    ones = ones_ref[...]
    k3 = jnp.concatenate([r[...] for r in kc_refs], axis=0)
    v3 = jnp.concatenate([r[...] for r in vc_refs], axis=0)
    _decode_update(k3, v3, q, ones, m_ref, l_ref, acc_ref, hd)

    @pl.when(p_idx == pl.num_programs(1) - 1)
    def _():
        _decode_update(kn_ref[...][None], vn_ref[...][None], q, ones, m_ref, l_ref, acc_ref, hd)
        lam = _lambda(lq1_ref, lk1_ref, lq2_ref, lk2_ref, lam_init)
        l = l_ref[...]
        l1 = jnp.concatenate([l[:, :hd], l[:, :hd]], axis=-1)
        l2 = jnp.concatenate([l[:, hd:], l[:, hd:]], axis=-1)
        o = acc_ref[0] / l1 - lam * (acc_ref[1] / l2)
        o_ref[...] = _subln(o, g_ref[...], lam_init)


def _decode_attn(page_table, q, cache_k, cache_v, k_new, v_new, lq1, lk1, lq2, lk2, subln_g, *, layer, lam_init,
                 pages):
    b, n_heads, hw = q.shape
    hd = hw // 2
    n_pages = page_table.shape[1]
    assert n_pages % pages == 0
    page = cache_k.shape[2]
    half = lax.broadcasted_iota(jnp.int32, (hw, hw), 0) // hd == lax.broadcasted_iota(jnp.int32, (hw, hw), 1) // hd
    ones = half.astype(BF16)
    per_seq = pl.BlockSpec((None, n_heads, hw), lambda i, p, pt: (i, 0, 0))
    paged = [pl.BlockSpec((None, None, page, n_heads, hw),
                          lambda i, p, pt, k=k: (layer, pt[i, p * pages + k], 0, 0, 0)) for k in range(pages)]
    vec = pl.BlockSpec((1, hd), lambda i, p, pt: (0, 0))
    grid_spec = pltpu.PrefetchScalarGridSpec(
        num_scalar_prefetch=1,
        grid=(b, n_pages // pages),
        in_specs=[per_seq] + paged + paged + [per_seq, per_seq,
                                               pl.BlockSpec((hw, hw), lambda i, p, pt: (0, 0)),
                                               vec, vec, vec, vec,
                                               pl.BlockSpec((1, hw), lambda i, p, pt: (0, 0))],
        out_specs=per_seq,
        scratch_shapes=[pltpu.VMEM((n_heads, hw), F32), pltpu.VMEM((n_heads, hw), F32),
                        pltpu.VMEM((2, n_heads, hw), F32)],
    )
    return pl.pallas_call(
        functools.partial(_decode_kernel, hd=hd, lam_init=lam_init, pages=pages),
        grid_spec=grid_spec,
        out_shape=jax.ShapeDtypeStruct((b, n_heads, hw), F32),
        compiler_params=_cparams(("parallel", "arbitrary")),
        name="decode_attn",
    )(page_table, q, *([cache_k] * pages), *([cache_v] * pages), k_new, v_new, ones, lq1, lk1, lq2, lk2, subln_g)


def _ssm_prep_kernel(ar_ref, ai_ref, ldt_ref, br_ref, bi_ref, abr_ref, abi_ref, bbr_ref, bbi_ref):
    ar = ar_ref[...]
    ai = ai_ref[...]
    dt = jnp.exp(ldt_ref[...])
    mag = jnp.exp(ar * dt)
    ang = ai * dt
    abr = mag * jnp.cos(ang)
    abi = mag * jnp.sin(ang)
    den = ar * ar + ai * ai
    nr = abr - 1.0
    ni = abi
    f_re = (nr * ar + ni * ai) / den
    f_im = (ni * ar - nr * ai) / den
    br = br_ref[...]
    bi = bi_ref[...]
    abr_ref[...] = abr
    abi_ref[...] = abi
    bbr_ref[...] = f_re * br - f_im * bi
    bbi_ref[...] = f_re * bi + f_im * br


def _ssm_prep(a_re, a_im, log_dt, b_re, b_im):
    g, p = a_re.shape
    j = b_re.shape[2]
    s = g * p
    row = lambda x: x.reshape(1, s)
    bt = lambda x: x.reshape(s, j).T
    return pl.pallas_call(
        _ssm_prep_kernel,
        out_shape=[jax.ShapeDtypeStruct((1, s), F32), jax.ShapeDtypeStruct((1, s), F32),
                   jax.ShapeDtypeStruct((j, s), F32), jax.ShapeDtypeStruct((j, s), F32)],
        name="ssm_prep",
    )(row(a_re), row(a_im), row(jnp.broadcast_to(log_dt[:, None], (g, p))), bt(b_re), bt(b_im))


def _block_diag_in(bb, n_groups, p):
    j = bb.shape[0]
    tg = SSM_TILE_GROUPS
    nt = n_groups // tg
    x = bb.reshape(j, nt, tg, p).transpose(1, 0, 2, 3)
    eye = jnp.eye(tg, dtype=bb.dtype)
    x = eye[None, :, None, :, None] * x[:, None]
    return x.reshape(nt, tg * j, tg * p)


def _block_diag_out(c, n_groups):
    _, j, p = c.shape
    tg = SSM_TILE_GROUPS
    nt = n_groups // tg
    x = c.reshape(nt, tg, j, p).transpose(0, 1, 3, 2)
    eye = jnp.eye(tg, dtype=c.dtype)
    x = x[:, :, :, None, :] * eye[None, :, None, :, None]
    return x.reshape(nt, tg * p, tg * j)


def _ssm_in_proj(u, bre_ref, bim_ref, bur_ref, bui_ref):
    nt, kin, nout = bre_ref.shape
    for t in range(nt):
        ut = u[:, t * kin:(t + 1) * kin].astype(BF16)
        bur_ref[:, t * nout:(t + 1) * nout] = jnp.dot(ut, bre_ref[t], preferred_element_type=F32)
        bui_ref[:, t * nout:(t + 1) * nout] = jnp.dot(ut, bim_ref[t], preferred_element_type=F32)


def _ssm_out(xr_ref, xi_ref, u, cre_ref, cim_ref, d_ref, wglu_ref, o_ref):
    nt, kin, nout = cre_ref.shape
    parts = []
    for t in range(nt):
        xr = xr_ref[:, t * kin:(t + 1) * kin].astype(BF16)
        xi = xi_ref[:, t * kin:(t + 1) * kin].astype(BF16)
        parts.append(jnp.dot(xr, cre_ref[t], preferred_element_type=F32)
                     - jnp.dot(xi, cim_ref[t], preferred_element_type=F32))
    y = jnp.concatenate(parts, axis=-1) + d_ref[...] * u
    y = _gelu_tanh(y)
    y = y * _sigmoid(jnp.dot(y.astype(BF16), wglu_ref[...], preferred_element_type=F32))
    o_ref[...] = y.astype(o_ref.dtype)


def _ssm_seq_kernel(u_ref, abr_ref, abi_ref, bre_ref, bim_ref, cre_ref, cim_ref, d_ref, wglu_ref,
                    o_ref, sr_ref, si_ref, xr_ref, xi_ref, cr_ref, ci_ref, *, lane_chunk):
    c = pl.program_id(1)
    t_len = u_ref.shape[0]
    n_state = abr_ref.shape[1]

    @pl.when(c == 0)
    def _():
        cr_ref[...] = jnp.zeros(cr_ref.shape, F32)
        ci_ref[...] = jnp.zeros(ci_ref.shape, F32)

    u = u_ref[...]
    _ssm_in_proj(u, bre_ref, bim_ref, xr_ref, xi_ref)

    for lc in range(n_state // lane_chunk):
        sl = slice(lc * lane_chunk, (lc + 1) * lane_chunk)
        ar = abr_ref[:, sl]
        ai = abi_ref[:, sl]

        def step(t, carry, sl=sl, ar=ar, ai=ai):
            xr, xi = carry
            nr = ar * xr - ai * xi + xr_ref[pl.ds(t, 1), sl]
            ni = ar * xi + ai * xr + xi_ref[pl.ds(t, 1), sl]
            xr_ref[pl.ds(t, 1), sl] = nr
            xi_ref[pl.ds(t, 1), sl] = ni
            return nr, ni

        xr, xi = lax.fori_loop(0, t_len, step, (cr_ref[:, sl], ci_ref[:, sl]), unroll=8)
        cr_ref[:, sl] = xr
        ci_ref[:, sl] = xi

    _ssm_out(xr_ref, xi_ref, u, cre_ref, cim_ref, d_ref, wglu_ref, o_ref)

    @pl.when(c == pl.num_programs(1) - 1)
    def _():
        sr_ref[...] = cr_ref[...]
        si_ref[...] = ci_ref[...]


def _ssm_seq(z3, abr, abi, bre, bim, cre, cim, d, wglu, *, d_ssm, u_col, lc_rows):
    b, t, _ = z3.shape
    n_state = abr.shape[1]
    full2 = lambda a: pl.BlockSpec(a.shape, lambda i, c: (0, 0))
    full3 = lambda a: pl.BlockSpec(a.shape, lambda i, c: (0, 0, 0))
    return pl.pallas_call(
        functools.partial(_ssm_seq_kernel, lane_chunk=1024),
        grid=(b, t // lc_rows),
        in_specs=[pl.BlockSpec((None, lc_rows, d_ssm), lambda i, c: (i, c, u_col)),
                  full2(abr), full2(abi), full3(bre), full3(bim), full3(cre), full3(cim), full2(d), full2(wglu)],
        out_specs=[pl.BlockSpec((None, lc_rows, d_ssm), lambda i, c: (i, c, 0)),
                   pl.BlockSpec((None, 1, n_state), lambda i, c: (i, 0, 0)),
                   pl.BlockSpec((None, 1, n_state), lambda i, c: (i, 0, 0))],
        out_shape=[jax.ShapeDtypeStruct((b, t, d_ssm), BF16),
                   jax.ShapeDtypeStruct((b, 1, n_state), F32),
                   jax.ShapeDtypeStruct((b, 1, n_state), F32)],
        scratch_shapes=[pltpu.VMEM((lc_rows, n_state), F32), pltpu.VMEM((lc_rows, n_state), F32),
                        pltpu.VMEM((1, n_state), F32), pltpu.VMEM((1, n_state), F32)],
        compiler_params=_cparams(("parallel", "arbitrary")),
        name="ssm_seq",
    )(z3, abr, abi, bre, bim, cre, cim, d, wglu)


def _ssm_step_kernel(u_ref, hr_ref, hi_ref, abr_ref, abi_ref, bre_ref, bim_ref, cre_ref, cim_ref, d_ref, wglu32_ref,
                     o_ref, sr_ref, si_ref, wglu_ref):
    wglu_ref[...] = wglu32_ref[...].astype(BF16)
    u = u_ref[...]
    _ssm_in_proj(u, bre_ref, bim_ref, sr_ref, si_ref)
    ar = abr_ref[...]
    ai = abi_ref[...]
    hr = hr_ref[...]
    hi = hi_ref[...]
    sr_ref[...] = ar * hr - ai * hi + sr_ref[...]
    si_ref[...] = ar * hi + ai * hr + si_ref[...]
    _ssm_out(sr_ref, si_ref, u, cre_ref, cim_ref, d_ref, wglu_ref, o_ref)


def _ssm_step(z, h_re, h_im, abr, abi, bre, bim, cre, cim, d, wglu, *, layer, d_ssm, u_col):
    b = z.shape[0]
    n_state = abr.shape[1]
    full2 = lambda a: pl.BlockSpec(a.shape, lambda i: (0, 0))
    full3 = lambda a: pl.BlockSpec(a.shape, lambda i: (0, 0, 0))
    return pl.pallas_call(
        _ssm_step_kernel,
        grid=(1,),
        in_specs=[pl.BlockSpec((b, d_ssm), lambda i: (0, u_col)),
                  full2(h_re), full2(h_im), full2(abr), full2(abi),
                  full3(bre), full3(bim), full3(cre), full3(cim), full2(d),
                  pl.BlockSpec((None, d_ssm, d_ssm), lambda i: (layer, 0, 0))],
        out_specs=[pl.BlockSpec((b, d_ssm), lambda i: (0, 0)),
                   pl.BlockSpec((b, n_state), lambda i: (0, 0)),
                   pl.BlockSpec((b, n_state), lambda i: (0, 0)),
                   pl.BlockSpec((d_ssm, d_ssm), lambda i: (0, 0))],
        out_shape=[jax.ShapeDtypeStruct((b, d_ssm), BF16),
                   jax.ShapeDtypeStruct((b, n_state), F32),
                   jax.ShapeDtypeStruct((b, n_state), F32),
                   jax.ShapeDtypeStruct((d_ssm, d_ssm), BF16)],
        compiler_params=_cparams(("arbitrary",)),
        name="ssm_step",
    )(z, h_re, h_im, abr, abi, bre, bim, cre, cim, d, wglu)


def _conv_post(yc_ref, cb_ref, lg_ref, lb_ref, hc_ref):
    rows = yc_ref.shape[0]
    chunk = _row_chunk(rows)
    bias = cb_ref[...]
    lg = lg_ref[...]
    lb = lb_ref[...]

    def body(c, carry):
        r0 = pl.multiple_of(c * chunk, chunk)
        y = yc_ref[pl.ds(r0, chunk), :] + bias
        mu = jnp.mean(y, axis=-1, keepdims=True)
        yc = y - mu
        var = jnp.mean(yc * yc, axis=-1, keepdims=True)
        hc_ref[pl.ds(r0, chunk), :] = _silu((yc * lax.rsqrt(var + EPS)) * lg + lb).astype(BF16)
        return carry

    lax.fori_loop(0, rows // chunk, body, 0, unroll=4)


def _merge_math(hc, att, ys, ga, gb, gc, wc, wa, ws):
    a = jnp.dot(hc, wc, preferred_element_type=F32)
    b = jnp.dot(att.astype(BF16), wa, preferred_element_type=F32)
    c = jnp.dot(ys, ws, preferred_element_type=F32)
    return (_sigmoid(ga) * a + _sigmoid(gb) * b + _sigmoid(gc) * c).astype(BF16)


def _merge_kernel(yc_ref, att_ref, ys_ref, ga_ref, gb_ref, gc_ref, cb_ref, lg_ref, lb_ref,
                  wc_ref, wa_ref, ws_ref, o_ref, hc_ref):
    @pl.when(pl.program_id(1) == 0)
    def _():
        _conv_post(yc_ref, cb_ref, lg_ref, lb_ref, hc_ref)

    o_ref[...] = _merge_math(hc_ref[...], att_ref[...], ys_ref[...], ga_ref[...], gb_ref[...], gc_ref[...],
                             wc_ref[...], wa_ref[...], ws_ref[...])


def _merge_cast_kernel(yc_ref, att_ref, ys_ref, ga_ref, gb_ref, gc_ref, cb_ref, lg_ref, lb_ref,
                       wc_ref, wa_ref, ws_ref, o_ref, wcb_ref, wab_ref, wsb_ref, hc_ref):
    @pl.when(pl.program_id(0) == 0)
    def _():
        _conv_post(yc_ref, cb_ref, lg_ref, lb_ref, hc_ref)

    wcb_ref[...] = wc_ref[...].astype(BF16)
    wab_ref[...] = wa_ref[...].astype(BF16)
    wsb_ref[...] = ws_ref[...].astype(BF16)
    o_ref[...] = _merge_math(hc_ref[...], att_ref[...], ys_ref[...], ga_ref[...], gb_ref[...], gc_ref[...],
                             wcb_ref[...], wab_ref[...], wsb_ref[...])


def _merge_cast(yconv, att, ys, z, conv_b, ln_g, ln_b, w_conv, w_attn, w_ssm, *, layer, gate_col, bn):
    m, d_conv = yconv.shape
    d_attn = att.shape[1]
    d_ssm = ys.shape[1]
    d = w_conv.shape[2]
    assert d % bn == 0 and gate_col % bn == 0
    g0 = gate_col // bn
    nj = d // bn
    vec = pl.BlockSpec((1, d_conv), lambda j: (0, 0))
    wspec = lambda k: pl.BlockSpec((None, k, bn), lambda j: (layer, 0, j))
    ospec = lambda k: pl.BlockSpec((k, bn), lambda j: (0, j))
    return pl.pallas_call(
        _merge_cast_kernel,
        grid=(nj,),
        in_specs=[
            pl.BlockSpec((m, d_conv), lambda j: (0, 0)),
            pl.BlockSpec((m, d_attn), lambda j: (0, 0)),
            pl.BlockSpec((m, d_ssm), lambda j: (0, 0)),
            pl.BlockSpec((m, bn), lambda j: (0, g0 + j)),
            pl.BlockSpec((m, bn), lambda j: (0, g0 + nj + j)),
            pl.BlockSpec((m, bn), lambda j: (0, g0 + 2 * nj + j)),
            vec, vec, vec,
            wspec(d_conv), wspec(d_attn), wspec(d_ssm),
        ],
        out_specs=[pl.BlockSpec((m, bn), lambda j: (0, j)), ospec(d_conv), ospec(d_attn), ospec(d_ssm)],
        out_shape=[jax.ShapeDtypeStruct((m, d), BF16), jax.ShapeDtypeStruct((d_conv, d), BF16),
                   jax.ShapeDtypeStruct((d_attn, d), BF16), jax.ShapeDtypeStruct((d_ssm, d), BF16)],
        scratch_shapes=[pltpu.VMEM((m, d_conv), BF16)],
        compiler_params=_cparams(("arbitrary",)),
        name="merge_cast",
    )(yconv, att, ys, z, z, z, conv_b, ln_g, ln_b, w_conv, w_attn, w_ssm)


def _merge(yconv, att, ys, z, conv_b, ln_g, ln_b, w_conv, w_attn, w_ssm, *, gate_col, bm, bn):
    m, d_conv = yconv.shape
    d_attn = att.shape[1]
    d_ssm = ys.shape[1]
    d = w_conv.shape[1]
    assert m % bm == 0 and d % bn == 0 and gate_col % bn == 0
    g0 = gate_col // bn
    nj = d // bn
    vec = pl.BlockSpec((1, d_conv), lambda i, j: (0, 0))
    return pl.pallas_call(
        _merge_kernel,
        grid=(m // bm, nj),
        in_specs=[
            pl.BlockSpec((bm, d_conv), lambda i, j: (i, 0)),
            pl.BlockSpec((bm, d_attn), lambda i, j: (i, 0)),
            pl.BlockSpec((bm, d_ssm), lambda i, j: (i, 0)),
            pl.BlockSpec((bm, bn), lambda i, j: (i, g0 + j)),
            pl.BlockSpec((bm, bn), lambda i, j: (i, g0 + nj + j)),
            pl.BlockSpec((bm, bn), lambda i, j: (i, g0 + 2 * nj + j)),
            vec, vec, vec,
            pl.BlockSpec((d_conv, bn), lambda i, j: (0, j)),
            pl.BlockSpec((d_attn, bn), lambda i, j: (0, j)),
            pl.BlockSpec((d_ssm, bn), lambda i, j: (0, j)),
        ],
        out_specs=pl.BlockSpec((bm, bn), lambda i, j: (i, j)),
        out_shape=jax.ShapeDtypeStruct((m, d), BF16),
        scratch_shapes=[pltpu.VMEM((bm, d_conv), BF16)],
        compiler_params=_cparams(("parallel", "arbitrary")),
        name="merge",
    )(yconv, att, ys, z, z, z, conv_b, ln_g, ln_b, w_conv, w_attn, w_ssm)


def _resproj_kernel(a_ref, w_ref, x_ref, o_ref):
    o_ref[...] = x_ref[...] + jnp.dot(a_ref[...], w_ref[...], preferred_element_type=F32)


def _resproj(a, w, x, *, bm, bn):
    m, k = a.shape
    n = w.shape[1]
    assert m % bm == 0 and n % bn == 0
    return pl.pallas_call(
        _resproj_kernel,
        grid=(m // bm, n // bn),
        in_specs=[
            pl.BlockSpec((bm, k), lambda i, j: (i, 0)),
            pl.BlockSpec((k, bn), lambda i, j: (0, j)),
            pl.BlockSpec((bm, bn), lambda i, j: (i, j)),
        ],
        out_specs=pl.BlockSpec((bm, bn), lambda i, j: (i, j)),
        out_shape=jax.ShapeDtypeStruct((m, n), F32),
        compiler_params=_cparams(("parallel", "arbitrary")),
        name="resproj",
    )(a, w, x)


def _resproj_cast_kernel(a_ref, w_ref, x_ref, o_ref, wb_ref):
    wb_ref[...] = w_ref[...].astype(BF16)
    o_ref[...] = x_ref[...] + jnp.dot(a_ref[...], wb_ref[...], preferred_element_type=F32)


def _resproj_cast(a, w, x, *, layer, bn):
    m, k = a.shape
    n = w.shape[2]
    assert n % bn == 0
    return pl.pallas_call(
        _resproj_cast_kernel,
        grid=(n // bn,),
        in_specs=[
            pl.BlockSpec((m, k), lambda j: (0, 0)),
            pl.BlockSpec((None, k, bn), lambda j: (layer, 0, j)),
            pl.BlockSpec((m, bn), lambda j: (0, j)),
        ],
        out_specs=[pl.BlockSpec((m, bn), lambda j: (0, j)), pl.BlockSpec((k, bn), lambda j: (0, j))],
        out_shape=[jax.ShapeDtypeStruct((m, n), F32), jax.ShapeDtypeStruct((k, n), BF16)],
        compiler_params=_cparams(("arbitrary",)),
        name="resproj_cast",
    )(a, w, x)


def _final_norm_kernel(x_ref, g_ref, o_ref):
    _rms_rows(x_ref, g_ref, o_ref)


def _final_norm(x, g, *, bm):
    m, d = x.shape
    assert m % bm == 0
    return pl.pallas_call(
        _final_norm_kernel,
        grid=(m // bm,),
        in_specs=[pl.BlockSpec((bm, d), lambda i: (i, 0)), pl.BlockSpec((1, d), lambda i: (0, 0))],
        out_specs=pl.BlockSpec((bm, d), lambda i: (i, 0)),
        out_shape=jax.ShapeDtypeStruct((m, d), F32),
        compiler_params=_cparams(("parallel",)),
        name="final_norm",
    )(x, g)


def kernel(x_prompt, x_sample, cache_k, cache_v, page_table, state_conv, state_ssm_re, state_ssm_im, meta_tokens, norm_ffn1, ffn1_gate, ffn1_up, ffn1_down, norm_mix, w_in, conv_w, conv_b, conv_ln_g, conv_ln_b, conv_proj, lambda_q1, lambda_k1, lambda_q2, lambda_k2, attn_subln_g, attn_proj, ssm_a_re, ssm_a_im, ssm_log_dt, ssm_b_re, ssm_b_im, ssm_c_re, ssm_c_im, ssm_d, ssm_glu, ssm_proj, w_out, norm_ffn2, ffn2_gate, ffn2_up, ffn2_down, norm_final):
    bp, seq, d = x_prompt.shape
    db = x_sample.shape[0]
    depth = w_in.shape[0]
    d_conv = conv_w.shape[2]
    hd = lambda_q1.shape[1]
    hw = 2 * hd
    n_heads = cache_k.shape[3]
    d_attn = n_heads * hw
    n_groups, n_state_g = ssm_a_re.shape[1], ssm_a_re.shape[2]
    d_ssm = ssm_d.shape[1]
    n_state = n_groups * n_state_g
    t = seq + N_META
    mp = bp * t
    q_off = 2 * d_conv
    k_off = q_off + d_attn
    v_off = k_off + d_attn
    u_off = v_off + d_attn
    gate_off = u_off + d_ssm

    bm_p = t // 3
    bm_ffn = t // 2
    row2 = lambda v: v.reshape(1, -1)

    xp = jnp.concatenate([jnp.broadcast_to(meta_tokens[None], (bp, N_META, d)), x_prompt], axis=1).reshape(mp, d)
    xs = x_sample.reshape(db, d)
    zero_conv = jnp.zeros((bp, CONV_PAD, d_conv), F32)

    outs_p = {k: [] for k in ("k", "v", "conv", "sre", "sim")}
    outs_s = {k: [] for k in ("k", "v", "conv", "sre", "sim")}
    for l in range(depth):
        lam_init = 0.8 - 0.6 * math.exp(-0.3 * l)
        lq1, lk1, lq2, lk2 = row2(lambda_q1[l]), row2(lambda_k1[l]), row2(lambda_q2[l]), row2(lambda_k2[l])
        sub_g = row2(attn_subln_g[l])
        cb, lng, lnb = row2(conv_b[l]), row2(conv_ln_g[l]), row2(conv_ln_b[l])

        abr, abi, bbr, bbi = _ssm_prep(ssm_a_re[l], ssm_a_im[l], ssm_log_dt[l], ssm_b_re[l], ssm_b_im[l])
        bre = _block_diag_in(bbr, n_groups, n_state_g).astype(BF16)
        bim = _block_diag_in(bbi, n_groups, n_state_g).astype(BF16)
        cre = _block_diag_out(ssm_c_re[l], n_groups).astype(BF16)
        cim = _block_diag_out(ssm_c_im[l], n_groups).astype(BF16)
        dvec = row2(ssm_d[l])

        xs, wg1, wu1, wd1 = _ffn_cast(xs, row2(norm_ffn1[l]), ffn1_gate, ffn1_up, ffn1_down, layer=l, bf=FFN_CAST_BF)
        xp = _ffn(xp, row2(norm_ffn1[l]), wg1, wu1, wd1, bm=bm_ffn, bf=FFN_BF)

        zs, win = _inproj_cast(xs, row2(norm_mix[l]), w_in, layer=l, bn=512)
        zp = _inproj(xp, row2(norm_mix[l]), win, bm=bm_p, bn=1024)
        zp3 = zp.reshape(bp, t, -1)

        yconv_s, conv_s = _conv_step(zs, state_conv[l], conv_w[l], d_conv=d_conv)
        q_s = zs[:, q_off:k_off].reshape(db, n_heads, hw)
        k_s = zs[:, k_off:v_off].reshape(db, n_heads, hw)
        v_s = zs[:, v_off:u_off].reshape(db, n_heads, hw)
        att_s = _decode_attn(page_table, q_s, cache_k, cache_v, k_s, v_s, lq1, lk1, lq2, lk2, sub_g,
                             layer=l, lam_init=lam_init, pages=DECODE_PAGES)
        ys_s, sre_s, sim_s, wglu = _ssm_step(zs, state_ssm_re[l].reshape(db, n_state),
                                             state_ssm_im[l].reshape(db, n_state), abr, abi, bre, bim, cre, cim, dvec,
                                             ssm_glu, layer=l, d_ssm=d_ssm, u_col=u_off // d_ssm)

        yconv_p, conv_p = _conv_seq(zp3, zero_conv, conv_w[l], d_conv=d_conv, cb=256, rows=48)
        att_p, k_p, v_p = _flash(zp3, lq1, lk1, lq2, lk2, sub_g, n_heads=n_heads, hd=hd, q_col=q_off // hw,
                                 k_col=k_off // hw, v_col=v_off // hw, tq=t // 3, tk=t // 3, lam_init=lam_init)
        outs_p["k"].append(k_p.reshape(bp, t, n_heads, hw))
        outs_p["v"].append(v_p.reshape(bp, t, n_heads, hw))
        ys_p, sre_p, sim_p = _ssm_seq(zp3, abr, abi, bre, bim, cre, cim, dvec, wglu,
                                      d_ssm=d_ssm, u_col=u_off // d_ssm, lc_rows=t // 3)

        merged_s, wconv, wattn, wssm = _merge_cast(yconv_s, att_s.reshape(db, d_attn), ys_s, zs, cb, lng, lnb,
                                                   conv_proj, attn_proj, ssm_proj, layer=l, gate_col=gate_off, bn=512)
        merged_p = _merge(yconv_p.reshape(mp, d_conv), att_p.reshape(mp, d_attn), ys_p.reshape(mp, d_ssm), zp,
                          cb, lng, lnb, wconv, wattn, wssm, gate_col=gate_off, bm=bm_p, bn=512)
        xs, wout = _resproj_cast(merged_s, w_out, xs, layer=l, bn=512)
        xp = _resproj(merged_p, wout, xp, bm=bm_p, bn=1024)

        xs, wg2, wu2, wd2 = _ffn_cast(xs, row2(norm_ffn2[l]), ffn2_gate, ffn2_up, ffn2_down, layer=l, bf=FFN_CAST_BF)
        xp = _ffn(xp, row2(norm_ffn2[l]), wg2, wu2, wd2, bm=bm_ffn, bf=FFN_BF)

        outs_p["conv"].append(conv_p)
        outs_p["sre"].append(sre_p.reshape(bp, n_groups, n_state_g))
        outs_p["sim"].append(sim_p.reshape(bp, n_groups, n_state_g))
        outs_s["k"].append(k_s.reshape(db, 1, n_heads, hw))
        outs_s["v"].append(v_s.reshape(db, 1, n_heads, hw))
        outs_s["conv"].append(conv_s)
        outs_s["sre"].append(sre_s.reshape(db, n_groups, n_state_g))
        outs_s["sim"].append(sim_s.reshape(db, n_groups, n_state_g))

    y_prompt = _final_norm(xp, row2(norm_final), bm=bm_p).reshape(bp, t, d)[:, N_META:]
    y_sample = _final_norm(xs, row2(norm_final), bm=db).reshape(db, 1, d)
    st = jnp.stack
    return (y_prompt, y_sample, st(outs_p["k"]), st(outs_p["v"]), st(outs_p["conv"]), st(outs_p["sre"]),
            st(outs_p["sim"]), st(outs_s["k"]), st(outs_s["v"]), st(outs_s["conv"]), st(outs_s["sre"]),
            st(outs_s["sim"]))
```

```python
import functools
import math

import jax
import jax.numpy as jnp
from jax import lax
from jax.experimental import pallas as pl
from jax.experimental.pallas import tpu as pltpu

F32 = jnp.float32
BF16 = jnp.bfloat16

EPS = 1e-6
NEG_BIG = -1e30
LOG2_E = 1.4426950408889634
CONV_WIDTH = 31
CONV_PAD = 32
N_META = 16
SSM_TILE_GROUPS = 8
FFN_BF = 256
FFN_CAST_BF = 256
DECODE_PAGES = 8
VMEM_LIMIT = 60 * 1024 * 1024


def _cparams(sem):
    return pltpu.CompilerParams(dimension_semantics=sem, vmem_limit_bytes=VMEM_LIMIT)


def _sigmoid(x):
    return 0.5 * jnp.tanh(0.5 * x) + 0.5


def _silu(x):
    return x * _sigmoid(x)


def _gelu_tanh(x):
    return 0.5 * x * (1.0 + jnp.tanh(math.sqrt(2.0 / math.pi) * (x + 0.044715 * (x * x * x))))


def _row_chunk(rows):
    for c in (16, 8):
        if rows % c == 0:
            return c
    raise ValueError(f"row count {rows} must be a multiple of 8")


def _rms_rows(x_ref, g_ref, h_ref):
    rows = x_ref.shape[0]
    assert rows % 8 == 0
    chunk = min(16, rows)
    g = g_ref[...]

    def norm(r0, n):
        x = x_ref[pl.ds(r0, n), :]
        ms = jnp.mean(x * x, axis=-1, keepdims=True)
        h_ref[pl.ds(r0, n), :] = ((x * lax.rsqrt(ms + EPS)) * g).astype(h_ref.dtype)

    def body(c, carry):
        norm(pl.multiple_of(c * chunk, chunk), chunk)
        return carry

    n_main = rows // chunk
    lax.fori_loop(0, n_main, body, 0, unroll=2)
    if rows > n_main * chunk:
        norm(n_main * chunk, rows - n_main * chunk)


def _row_splits(rows, parts):
    n_units = max(rows // 16, 1)
    parts = min(parts, n_units)
    base, extra = divmod(n_units, parts)
    out, r0 = [], 0
    for k in range(parts):
        n = (base + (1 if k < extra else 0)) * 16
        if k == parts - 1:
            n = rows - r0
        out.append((r0, n))
        r0 += n
    return tuple(out)


def _ffn_math(h, wg, wu, wd):
    gate = jnp.dot(h, wg, preferred_element_type=F32)
    up = jnp.dot(h, wu, preferred_element_type=F32)
    a = (0.5 * (_silu(gate) * up)).astype(BF16)
    return jnp.dot(a, wd, preferred_element_type=F32)


def _ffn_kernel(x_hbm, g_ref, wg_ref, wu_ref, wd_ref, o_ref, h_ref, sem, *, bm, splits):
    i = pl.program_id(0)
    f = pl.program_id(1)

    @pl.when(f == 0)
    def _():
        copies = [pltpu.make_async_copy(x_hbm.at[pl.ds(i * bm + r0, n), :], o_ref.at[pl.ds(r0, n), :], sem.at[k])
                  for k, (r0, n) in enumerate(splits)]
        for cp in copies:
            cp.start()
        for cp, (r0, n) in zip(copies, splits):
            cp.wait()
            _rms_rows(o_ref.at[pl.ds(r0, n), :], g_ref, h_ref.at[pl.ds(r0, n), :])

    o_ref[...] += _ffn_math(h_ref[...], wg_ref[...], wu_ref[...], wd_ref[...])


def _ffn(x, g, wg, wu, wd, *, bm, bf):
    m, d = x.shape
    f_dim = wg.shape[1]
    assert m % bm == 0 and f_dim % bf == 0
    splits = _row_splits(bm, 4)
    return pl.pallas_call(
        functools.partial(_ffn_kernel, bm=bm, splits=splits),
        grid=(m // bm, f_dim // bf),
        in_specs=[
            pl.BlockSpec(memory_space=pl.ANY),
            pl.BlockSpec((1, d), lambda i, f: (0, 0)),
            pl.BlockSpec((d, bf), lambda i, f: (0, f)),
            pl.BlockSpec((d, bf), lambda i, f: (0, f)),
            pl.BlockSpec((bf, d), lambda i, f: (f, 0)),
        ],
        out_specs=pl.BlockSpec((bm, d), lambda i, f: (i, 0)),
        out_shape=jax.ShapeDtypeStruct((m, d), F32),
        scratch_shapes=[pltpu.VMEM((bm, d), BF16), pltpu.SemaphoreType.DMA((len(splits),))],
        compiler_params=_cparams(("parallel", "arbitrary")),
        name="ffn",
    )(x, g, wg, wu, wd)


def _ffn_cast_kernel(x_ref, g_ref, wg_ref, wu_ref, wd_ref, o_ref, wgb_ref, wub_ref, wdb_ref, h_ref):
    @pl.when(pl.program_id(0) == 0)
    def _():
        _rms_rows(x_ref, g_ref, h_ref)
        o_ref[...] = x_ref[...]

    wgb_ref[...] = wg_ref[...].astype(BF16)
    wub_ref[...] = wu_ref[...].astype(BF16)
    wdb_ref[...] = wd_ref[...].astype(BF16)
    o_ref[...] += _ffn_math(h_ref[...], wgb_ref[...], wub_ref[...], wdb_ref[...])


def _ffn_cast(x, g, wg, wu, wd, *, layer, bf):
    m, d = x.shape
    f_dim = wg.shape[2]
    assert f_dim % bf == 0
    return pl.pallas_call(
        _ffn_cast_kernel,
        grid=(f_dim // bf,),
        in_specs=[
            pl.BlockSpec((m, d), lambda f: (0, 0)),
            pl.BlockSpec((1, d), lambda f: (0, 0)),
            pl.BlockSpec((None, d, bf), lambda f: (layer, 0, f)),
            pl.BlockSpec((None, d, bf), lambda f: (layer, 0, f)),
            pl.BlockSpec((None, bf, d), lambda f: (layer, f, 0)),
        ],
        out_specs=[
            pl.BlockSpec((m, d), lambda f: (0, 0)),
            pl.BlockSpec((d, bf), lambda f: (0, f)),
            pl.BlockSpec((d, bf), lambda f: (0, f)),
            pl.BlockSpec((bf, d), lambda f: (f, 0)),
        ],
        out_shape=[
            jax.ShapeDtypeStruct((m, d), F32),
            jax.ShapeDtypeStruct((d, f_dim), BF16),
            jax.ShapeDtypeStruct((d, f_dim), BF16),
            jax.ShapeDtypeStruct((f_dim, d), BF16),
        ],
        scratch_shapes=[pltpu.VMEM((m, d), BF16)],
        compiler_params=_cparams(("arbitrary",)),
        name="ffn_cast",
    )(x, g, wg, wu, wd)


def _inproj_kernel(x_ref, g_ref, w_ref, o_ref, h_ref):
    @pl.when(pl.program_id(1) == 0)
    def _():
        _rms_rows(x_ref, g_ref, h_ref)

    o_ref[...] = jnp.dot(h_ref[...], w_ref[...], preferred_element_type=F32)


def _inproj(x, g, w, *, bm, bn):
    m, d = x.shape
    n = w.shape[1]
    assert m % bm == 0 and n % bn == 0
    return pl.pallas_call(
        _inproj_kernel,
        grid=(m // bm, n // bn),
        in_specs=[
            pl.BlockSpec((bm, d), lambda i, j: (i, 0)),
            pl.BlockSpec((1, d), lambda i, j: (0, 0)),
            pl.BlockSpec((d, bn), lambda i, j: (0, j)),
        ],
        out_specs=pl.BlockSpec((bm, bn), lambda i, j: (i, j)),
        out_shape=jax.ShapeDtypeStruct((m, n), F32),
        scratch_shapes=[pltpu.VMEM((bm, d), BF16)],
        compiler_params=_cparams(("parallel", "arbitrary")),
        name="inproj",
    )(x, g, w)


def _inproj_cast_kernel(x_ref, g_ref, w_ref, o_ref, wb_ref, h_ref):
    @pl.when(pl.program_id(0) == 0)
    def _():
        _rms_rows(x_ref, g_ref, h_ref)

    wb_ref[...] = w_ref[...].astype(BF16)
    o_ref[...] = jnp.dot(h_ref[...], wb_ref[...], preferred_element_type=F32)


def _inproj_cast(x, g, w, *, layer, bn):
    m, d = x.shape
    n = w.shape[2]
    assert n % bn == 0
    return pl.pallas_call(
        _inproj_cast_kernel,
        grid=(n // bn,),
        in_specs=[
            pl.BlockSpec((m, d), lambda j: (0, 0)),
            pl.BlockSpec((1, d), lambda j: (0, 0)),
            pl.BlockSpec((None, d, bn), lambda j: (layer, 0, j)),
        ],
        out_specs=[pl.BlockSpec((m, bn), lambda j: (0, j)), pl.BlockSpec((d, bn), lambda j: (0, j))],
        out_shape=[jax.ShapeDtypeStruct((m, n), F32), jax.ShapeDtypeStruct((d, n), BF16)],
        scratch_shapes=[pltpu.VMEM((m, d), BF16)],
        compiler_params=_cparams(("arbitrary",)),
        name="inproj_cast",
    )(x, g, w)


def _conv_seq_kernel(a_ref, gt_ref, st_ref, w_ref, y_ref, ns_ref, ext_ref, *, rows):
    t_len = a_ref.shape[0]
    n_chunks = t_len // rows
    ext_ref[0:CONV_PAD, :] = st_ref[...]

    def glu(c, carry):
        r0 = pl.multiple_of(c * rows, rows)
        ext_ref[pl.ds(CONV_PAD + r0, rows), :] = a_ref[pl.ds(r0, rows), :] * _sigmoid(gt_ref[pl.ds(r0, rows), :])
        return carry

    lax.fori_loop(0, n_chunks, glu, 0)
    w = w_ref[...]
    off = CONV_PAD - (CONV_WIDTH - 1)

    def conv(c, carry):
        r0 = pl.multiple_of(c * rows, rows)
        win = ext_ref[pl.ds(r0, rows + CONV_PAD), :]
        shifted = [win] + [win[r:r + rows + CONV_PAD - 8, :] for r in range(1, 8)]
        acc = None
        for k in range(CONV_WIDTH):
            s = off + k
            tap = w[k:k + 1, :] * shifted[s % 8][8 * (s // 8):8 * (s // 8) + rows, :]
            acc = tap if acc is None else acc + tap
        y_ref[pl.ds(r0, rows), :] = acc
        return carry

    lax.fori_loop(0, n_chunks, conv, 0)
    ns_ref[...] = ext_ref[pl.ds(t_len + off, CONV_WIDTH - 1), :]


def _conv_seq(z3, state_pad, conv_w, *, d_conv, cb, rows):
    b, t, _ = z3.shape
    nj = d_conv // cb
    return pl.pallas_call(
        functools.partial(_conv_seq_kernel, rows=rows),
        grid=(b, nj),
        in_specs=[
            pl.BlockSpec((None, t, cb), lambda i, j: (i, 0, j)),
            pl.BlockSpec((None, t, cb), lambda i, j: (i, 0, nj + j)),
            pl.BlockSpec((None, CONV_PAD, cb), lambda i, j: (i, 0, j)),
            pl.BlockSpec((CONV_WIDTH, cb), lambda i, j: (0, j)),
        ],
        out_specs=[
            pl.BlockSpec((None, t, cb), lambda i, j: (i, 0, j)),
            pl.BlockSpec((None, CONV_WIDTH - 1, cb), lambda i, j: (i, 0, j)),
        ],
        out_shape=[
            jax.ShapeDtypeStruct((b, t, d_conv), F32),
            jax.ShapeDtypeStruct((b, CONV_WIDTH - 1, d_conv), F32),
        ],
        scratch_shapes=[pltpu.VMEM((CONV_PAD + t, cb), F32)],
        compiler_params=_cparams(("parallel", "parallel")),
        name="conv_seq",
    )(z3, z3, state_pad, conv_w)


def _conv_step_kernel(a_ref, gt_ref, st_ref, w_ref, y_ref, ns_ref):
    u = a_ref[...] * _sigmoid(gt_ref[...])
    st = st_ref[...]
    w = w_ref[...]
    y_ref[...] = jnp.sum(st * w[None, 0:CONV_WIDTH - 1, :], axis=1) + u * w[CONV_WIDTH - 1:CONV_WIDTH, :]
    ns_ref[:, 0:CONV_WIDTH - 2, :] = st[:, 1:CONV_WIDTH - 1, :]
    ns_ref[:, CONV_WIDTH - 2:CONV_WIDTH - 1, :] = u[:, None, :]


def _conv_step(z, state, conv_w, *, d_conv):
    b = z.shape[0]
    return pl.pallas_call(
        _conv_step_kernel,
        grid=(1,),
        in_specs=[
            pl.BlockSpec((b, d_conv), lambda i: (0, 0)),
            pl.BlockSpec((b, d_conv), lambda i: (0, 1)),
            pl.BlockSpec((b, CONV_WIDTH - 1, d_conv), lambda i: (0, 0, 0)),
            pl.BlockSpec((CONV_WIDTH, d_conv), lambda i: (0, 0)),
        ],
        out_specs=[
            pl.BlockSpec((b, d_conv), lambda i: (0, 0)),
            pl.BlockSpec((b, CONV_WIDTH - 1, d_conv), lambda i: (0, 0, 0)),
        ],
        out_shape=[
            jax.ShapeDtypeStruct((b, d_conv), F32),
            jax.ShapeDtypeStruct((b, CONV_WIDTH - 1, d_conv), F32),
        ],
        compiler_params=_cparams(("arbitrary",)),
        name="conv_step",
    )(z, z, state, conv_w)


def _lambda(lq1_ref, lk1_ref, lq2_ref, lk2_ref, lam_init):
    s1 = jnp.sum(lq1_ref[...] * lk1_ref[...], axis=-1, keepdims=True)
    s2 = jnp.sum(lq2_ref[...] * lk2_ref[...], axis=-1, keepdims=True)
    return jnp.exp(s1) - jnp.exp(s2) + lam_init


def _subln(o, g, lam_init):
    ms = jnp.mean(o * o, axis=-1, keepdims=True)
    return ((o * lax.rsqrt(ms + EPS)) * g) * (1.0 - lam_init)


def _flash_kernel(q_ref, k_ref, v_ref, lq1_ref, lk1_ref, lq2_ref, lk2_ref, g_ref, o_ref, ko_ref, vo_ref,
                  m_ref, l_ref, acc_ref, *, tq, tk, hd, scale, lam_init):
    qi = pl.program_id(2)

    @pl.when(qi == 0)
    def _():
        ko_ref[...] = k_ref[...]
        vo_ref[...] = v_ref[...]

    q = q_ref[...] * (LOG2_E * scale)
    qs = (q[:, :hd].astype(BF16), q[:, hd:].astype(BF16))
    m_ref[...] = jnp.full(m_ref.shape, NEG_BIG, F32)
    l_ref[...] = jnp.zeros(l_ref.shape, F32)
    acc_ref[...] = jnp.zeros(acc_ref.shape, F32)
    row = qi * tq + lax.broadcasted_iota(jnp.int32, (tq, tk), 0)
    col0 = lax.broadcasted_iota(jnp.int32, (tq, tk), 1)
    n_kv = (qi * tq + tq + tk - 1) // tk
    n_full = (qi * tq + 1) // tk

    def body(j, carry, *, masked):
        k0 = pl.multiple_of(j * tk, tk)
        k = k_ref[pl.ds(k0, tk), :]
        v = v_ref[pl.ds(k0, tk), :].astype(BF16)
        for c in range(2):
            kc = k[:, c * hd:(c + 1) * hd].astype(BF16)
            s = lax.dot_general(qs[c], kc, (((1,), (1,)), ((), ())), preferred_element_type=F32)
            if masked:
                s = jnp.where((col0 + k0) <= row, s, NEG_BIG)
            m_old = m_ref[c]
            m_new = jnp.maximum(m_old, jnp.max(s, axis=-1, keepdims=True))
            alpha = jnp.exp2(m_old - m_new)
            p = jnp.exp2(s - m_new)
            l_ref[c] = alpha * l_ref[c] + jnp.sum(p, axis=-1, keepdims=True)
            acc_ref[c] = alpha * acc_ref[c] + jnp.dot(p.astype(BF16), v, preferred_element_type=F32)
            m_ref[c] = m_new
        return carry

    lax.fori_loop(0, n_full, functools.partial(body, masked=False), 0)
    lax.fori_loop(n_full, n_kv, functools.partial(body, masked=True), 0)
    lam = _lambda(lq1_ref, lk1_ref, lq2_ref, lk2_ref, lam_init)
    o = acc_ref[0] / l_ref[0] - lam * (acc_ref[1] / l_ref[1])
    o_ref[...] = _subln(o, g_ref[...], lam_init).astype(o_ref.dtype)


def _flash(z3, lq1, lk1, lq2, lk2, subln_g, *, n_heads, hd, q_col, k_col, v_col, tq, tk, lam_init):
    b, t, _ = z3.shape
    hw = 2 * hd
    vec = pl.BlockSpec((1, hd), lambda i, h, j: (0, 0))
    slab = pl.BlockSpec((None, t, hw), lambda i, h, j: (i, 0, h))
    kv_shape = jax.ShapeDtypeStruct((b, t, n_heads * hw), F32)
    return pl.pallas_call(
        functools.partial(_flash_kernel, tq=tq, tk=tk, hd=hd, scale=1.0 / math.sqrt(hd), lam_init=lam_init),
        grid=(b, n_heads, t // tq),
        in_specs=[
            pl.BlockSpec((None, tq, hw), lambda i, h, j: (i, j, q_col + h)),
            pl.BlockSpec((None, t, hw), lambda i, h, j: (i, 0, k_col + h)),
            pl.BlockSpec((None, t, hw), lambda i, h, j: (i, 0, v_col + h)),
            vec, vec, vec, vec,
            pl.BlockSpec((1, hw), lambda i, h, j: (0, 0)),
        ],
        out_specs=[pl.BlockSpec((None, tq, hw), lambda i, h, j: (i, j, h)), slab, slab],
        out_shape=[jax.ShapeDtypeStruct((b, t, n_heads * hw), BF16), kv_shape, kv_shape],
        scratch_shapes=[pltpu.VMEM((2, tq, 1), F32), pltpu.VMEM((2, tq, 1), F32), pltpu.VMEM((2, tq, hw), F32)],
        compiler_params=_cparams(("parallel", "parallel", "arbitrary")),
        name="flash",
    )(z3, z3, z3, lq1, lk1, lq2, lk2, subln_g)


def _decode_update(k3, v3, q, ones, m_ref, l_ref, acc_ref, hd):
    n, n_heads, hw = k3.shape
    qs = q * (LOG2_E / math.sqrt(hd))
    prod = (k3 * qs[None]).reshape(n * n_heads, hw).astype(BF16)
    s = jnp.dot(prod, ones, preferred_element_type=F32).reshape(n, n_heads, hw)
    m_old = m_ref[...]
    m_new = jnp.maximum(m_old, jnp.max(s, axis=0))
    alpha = jnp.exp2(m_old - m_new)
    p = jnp.exp2(s - m_new[None])
    l_ref[...] = alpha * l_ref[...] + jnp.sum(p, axis=0)
    m_ref[...] = m_new
    for c in range(2):
        pc = p[:, :, c * hd:(c + 1) * hd]
        pw = jnp.concatenate([pc, pc], axis=-1)
        ac = alpha[:, c * hd:(c + 1) * hd]
        acc_ref[c] = jnp.concatenate([ac, ac], axis=-1) * acc_ref[c] + jnp.sum(pw * v3, axis=0)


def _decode_kernel(pt_ref, q_ref, *refs, hd, lam_init, pages):
    del pt_ref
    kc_refs, vc_refs = refs[:pages], refs[pages:2 * pages]
    (kn_ref, vn_ref, ones_ref, lq1_ref, lk1_ref, lq2_ref, lk2_ref, g_ref,
     o_ref, m_ref, l_ref, acc_ref) = refs[2 * pages:]
    p_idx = pl.program_id(1)

    @pl.when(p_idx == 0)
    def _():
        m_ref[...] = jnp.full(m_ref.shape, NEG_BIG, F32)
        l_ref[...] = jnp.zeros(l_ref.shape, F32)
        acc_ref[...] = jnp.zeros(acc_ref.shape, F32)

    q = q_ref[...]
    ones = ones_ref[...]
    k3 = jnp.concatenate([r[...] for r in kc_refs], axis=0)
    v3 = jnp.concatenate([r[...] for r in vc_refs], axis=0)
    _decode_update(k3, v3, q, ones, m_ref, l_ref, acc_ref, hd)

    @pl.when(p_idx == pl.num_programs(1) - 1)
    def _():
        _decode_update(kn_ref[...][None], vn_ref[...][None], q, ones, m_ref, l_ref, acc_ref, hd)
        lam = _lambda(lq1_ref, lk1_ref, lq2_ref, lk2_ref, lam_init)
        l = l_ref[...]
        l1 = jnp.concatenate([l[:, :hd], l[:, :hd]], axis=-1)
        l2 = jnp.concatenate([l[:, hd:], l[:, hd:]], axis=-1)
        o = acc_ref[0] / l1 - lam * (acc_ref[1] / l2)
        o_ref[...] = _subln(o, g_ref[...], lam_init)


def _decode_attn(page_table, q, cache_k, cache_v, k_new, v_new, lq1, lk1, lq2, lk2, subln_g, *, layer, lam_init,
                 pages):
    b, n_heads, hw = q.shape
    hd = hw // 2
    n_pages = page_table.shape[1]
    assert n_pages % pages == 0
    page = cache_k.shape[2]
    half = lax.broadcasted_iota(jnp.int32, (hw, hw), 0) // hd == lax.broadcasted_iota(jnp.int32, (hw, hw), 1) // hd
    ones = half.astype(BF16)
    per_seq = pl.BlockSpec((None, n_heads, hw), lambda i, p, pt: (i, 0, 0))
    paged = [pl.BlockSpec((None, None, page, n_heads, hw),
                          lambda i, p, pt, k=k: (layer, pt[i, p * pages + k], 0, 0, 0)) for k in range(pages)]
    vec = pl.BlockSpec((1, hd), lambda i, p, pt: (0, 0))
    grid_spec = pltpu.PrefetchScalarGridSpec(
        num_scalar_prefetch=1,
        grid=(b, n_pages // pages),
        in_specs=[per_seq] + paged + paged + [per_seq, per_seq,
                                               pl.BlockSpec((hw, hw), lambda i, p, pt: (0, 0)),
                                               vec, vec, vec, vec,
                                               pl.BlockSpec((1, hw), lambda i, p, pt: (0, 0))],
        out_specs=per_seq,
        scratch_shapes=[pltpu.VMEM((n_heads, hw), F32), pltpu.VMEM((n_heads, hw), F32),
                        pltpu.VMEM((2, n_heads, hw), F32)],
    )
    return pl.pallas_call(
        functools.partial(_decode_kernel, hd=hd, lam_init=lam_init, pages=pages),
        grid_spec=grid_spec,
        out_shape=jax.ShapeDtypeStruct((b, n_heads, hw), F32),
        compiler_params=_cparams(("parallel", "arbitrary")),
        name="decode_attn",
    )(page_table, q, *([cache_k] * pages), *([cache_v] * pages), k_new, v_new, ones, lq1, lk1, lq2, lk2, subln_g)


def _ssm_prep_kernel(ar_ref, ai_ref, ldt_ref, br_ref, bi_ref, abr_ref, abi_ref, bbr_ref, bbi_ref):
    ar = ar_ref[...]
    ai = ai_ref[...]
    dt = jnp.exp(ldt_ref[...])
    mag = jnp.exp(ar * dt)
    ang = ai * dt
    abr = mag * jnp.cos(ang)
    abi = mag * jnp.sin(ang)
    den = ar * ar + ai * ai
    nr = abr - 1.0
    ni = abi
    f_re = (nr * ar + ni * ai) / den
    f_im = (ni * ar - nr * ai) / den
    br = br_ref[...]
    bi = bi_ref[...]
    abr_ref[...] = abr
    abi_ref[...] = abi
    bbr_ref[...] = f_re * br - f_im * bi
    bbi_ref[...] = f_re * bi + f_im * br


def _ssm_prep(a_re, a_im, log_dt, b_re, b_im):
    g, p = a_re.shape
    j = b_re.shape[2]
    s = g * p
    row = lambda x: x.reshape(1, s)
    bt = lambda x: x.reshape(s, j).T
    return pl.pallas_call(
        _ssm_prep_kernel,
        out_shape=[jax.ShapeDtypeStruct((1, s), F32), jax.ShapeDtypeStruct((1, s), F32),
                   jax.ShapeDtypeStruct((j, s), F32), jax.ShapeDtypeStruct((j, s), F32)],
        name="ssm_prep",
    )(row(a_re), row(a_im), row(jnp.broadcast_to(log_dt[:, None], (g, p))), bt(b_re), bt(b_im))


def _block_diag_in(bb, n_groups, p):
    j = bb.shape[0]
    tg = SSM_TILE_GROUPS
    nt = n_groups // tg
    x = bb.reshape(j, nt, tg, p).transpose(1, 0, 2, 3)
    eye = jnp.eye(tg, dtype=bb.dtype)
    x = eye[None, :, None, :, None] * x[:, None]
    return x.reshape(nt, tg * j, tg * p)


def _block_diag_out(c, n_groups):
    _, j, p = c.shape
    tg = SSM_TILE_GROUPS
    nt = n_groups // tg
    x = c.reshape(nt, tg, j, p).transpose(0, 1, 3, 2)
    eye = jnp.eye(tg, dtype=c.dtype)
    x = x[:, :, :, None, :] * eye[None, :, None, :, None]
    return x.reshape(nt, tg * p, tg * j)


def _ssm_in_proj(u, bre_ref, bim_ref, bur_ref, bui_ref):
    nt, kin, nout = bre_ref.shape
    for t in range(nt):
        ut = u[:, t * kin:(t + 1) * kin].astype(BF16)
        bur_ref[:, t * nout:(t + 1) * nout] = jnp.dot(ut, bre_ref[t], preferred_element_type=F32)
        bui_ref[:, t * nout:(t + 1) * nout] = jnp.dot(ut, bim_ref[t], preferred_element_type=F32)


def _ssm_out(xr_ref, xi_ref, u, cre_ref, cim_ref, d_ref, wglu_ref, o_ref):
    nt, kin, nout = cre_ref.shape
    parts = []
    for t in range(nt):
        xr = xr_ref[:, t * kin:(t + 1) * kin].astype(BF16)
        xi = xi_ref[:, t * kin:(t + 1) * kin].astype(BF16)
        parts.append(jnp.dot(xr, cre_ref[t], preferred_element_type=F32)
                     - jnp.dot(xi, cim_ref[t], preferred_element_type=F32))
    y = jnp.concatenate(parts, axis=-1) + d_ref[...] * u
    y = _gelu_tanh(y)
    y = y * _sigmoid(jnp.dot(y.astype(BF16), wglu_ref[...], preferred_element_type=F32))
    o_ref[...] = y.astype(o_ref.dtype)


def _ssm_seq_kernel(u_ref, abr_ref, abi_ref, bre_ref, bim_ref, cre_ref, cim_ref, d_ref, wglu_ref,
                    o_ref, sr_ref, si_ref, xr_ref, xi_ref, cr_ref, ci_ref, *, lane_chunk):
    c = pl.program_id(1)
    t_len = u_ref.shape[0]
    n_state = abr_ref.shape[1]

    @pl.when(c == 0)
    def _():
        cr_ref[...] = jnp.zeros(cr_ref.shape, F32)
        ci_ref[...] = jnp.zeros(ci_ref.shape, F32)

    u = u_ref[...]
    _ssm_in_proj(u, bre_ref, bim_ref, xr_ref, xi_ref)

    for lc in range(n_state // lane_chunk):
        sl = slice(lc * lane_chunk, (lc + 1) * lane_chunk)
        ar = abr_ref[:, sl]
        ai = abi_ref[:, sl]

        def step(t, carry, sl=sl, ar=ar, ai=ai):
            xr, xi = carry
            nr = ar * xr - ai * xi + xr_ref[pl.ds(t, 1), sl]
            ni = ar * xi + ai * xr + xi_ref[pl.ds(t, 1), sl]
            xr_ref[pl.ds(t, 1), sl] = nr
            xi_ref[pl.ds(t, 1), sl] = ni
            return nr, ni

        xr, xi = lax.fori_loop(0, t_len, step, (cr_ref[:, sl], ci_ref[:, sl]), unroll=8)
        cr_ref[:, sl] = xr
        ci_ref[:, sl] = xi

    _ssm_out(xr_ref, xi_ref, u, cre_ref, cim_ref, d_ref, wglu_ref, o_ref)

    @pl.when(c == pl.num_programs(1) - 1)
    def _():
        sr_ref[...] = cr_ref[...]
        si_ref[...] = ci_ref[...]


def _ssm_seq(z3, abr, abi, bre, bim, cre, cim, d, wglu, *, d_ssm, u_col, lc_rows):
    b, t, _ = z3.shape
    n_state = abr.shape[1]
    full2 = lambda a: pl.BlockSpec(a.shape, lambda i, c: (0, 0))
    full3 = lambda a: pl.BlockSpec(a.shape, lambda i, c: (0, 0, 0))
    return pl.pallas_call(
        functools.partial(_ssm_seq_kernel, lane_chunk=1024),
        grid=(b, t // lc_rows),
        in_specs=[pl.BlockSpec((None, lc_rows, d_ssm), lambda i, c: (i, c, u_col)),
                  full2(abr), full2(abi), full3(bre), full3(bim), full3(cre), full3(cim), full2(d), full2(wglu)],
        out_specs=[pl.BlockSpec((None, lc_rows, d_ssm), lambda i, c: (i, c, 0)),
                   pl.BlockSpec((None, 1, n_state), lambda i, c: (i, 0, 0)),
                   pl.BlockSpec((None, 1, n_state), lambda i, c: (i, 0, 0))],
        out_shape=[jax.ShapeDtypeStruct((b, t, d_ssm), BF16),
                   jax.ShapeDtypeStruct((b, 1, n_state), F32),
                   jax.ShapeDtypeStruct((b, 1, n_state), F32)],
        scratch_shapes=[pltpu.VMEM((lc_rows, n_state), F32), pltpu.VMEM((lc_rows, n_state), F32),
                        pltpu.VMEM((1, n_state), F32), pltpu.VMEM((1, n_state), F32)],
        compiler_params=_cparams(("parallel", "arbitrary")),
        name="ssm_seq",
    )(z3, abr, abi, bre, bim, cre, cim, d, wglu)


def _ssm_step_kernel(u_ref, hr_ref, hi_ref, abr_ref, abi_ref, bre_ref, bim_ref, cre_ref, cim_ref, d_ref, wglu32_ref,
                     o_ref, sr_ref, si_ref, wglu_ref):
    wglu_ref[...] = wglu32_ref[...].astype(BF16)
    u = u_ref[...]
    _ssm_in_proj(u, bre_ref, bim_ref, sr_ref, si_ref)
    ar = abr_ref[...]
    ai = abi_ref[...]
    hr = hr_ref[...]
    hi = hi_ref[...]
    sr_ref[...] = ar * hr - ai * hi + sr_ref[...]
    si_ref[...] = ar * hi + ai * hr + si_ref[...]
    _ssm_out(sr_ref, si_ref, u, cre_ref, cim_ref, d_ref, wglu_ref, o_ref)


def _ssm_step(z, h_re, h_im, abr, abi, bre, bim, cre, cim, d, wglu, *, layer, d_ssm, u_col):
    b = z.shape[0]
    n_state = abr.shape[1]
    full2 = lambda a: pl.BlockSpec(a.shape, lambda i: (0, 0))
    full3 = lambda a: pl.BlockSpec(a.shape, lambda i: (0, 0, 0))
    return pl.pallas_call(
        _ssm_step_kernel,
        grid=(1,),
        in_specs=[pl.BlockSpec((b, d_ssm), lambda i: (0, u_col)),
                  full2(h_re), full2(h_im), full2(abr), full2(abi),
                  full3(bre), full3(bim), full3(cre), full3(cim), full2(d),
                  pl.BlockSpec((None, d_ssm, d_ssm), lambda i: (layer, 0, 0))],
        out_specs=[pl.BlockSpec((b, d_ssm), lambda i: (0, 0)),
                   pl.BlockSpec((b, n_state), lambda i: (0, 0)),
                   pl.BlockSpec((b, n_state), lambda i: (0, 0)),
                   pl.BlockSpec((d_ssm, d_ssm), lambda i: (0, 0))],
        out_shape=[jax.ShapeDtypeStruct((b, d_ssm), BF16),
                   jax.ShapeDtypeStruct((b, n_state), F32),
                   jax.ShapeDtypeStruct((b, n_state), F32),
                   jax.ShapeDtypeStruct((d_ssm, d_ssm), BF16)],
        compiler_params=_cparams(("arbitrary",)),
        name="ssm_step",
    )(z, h_re, h_im, abr, abi, bre, bim, cre, cim, d, wglu)


def _conv_post(yc_ref, cb_ref, lg_ref, lb_ref, hc_ref):
    rows = yc_ref.shape[0]
    chunk = _row_chunk(rows)
    bias = cb_ref[...]
    lg = lg_ref[...]
    lb = lb_ref[...]

    def body(c, carry):
        r0 = pl.multiple_of(c * chunk, chunk)
        y = yc_ref[pl.ds(r0, chunk), :] + bias
        mu = jnp.mean(y, axis=-1, keepdims=True)
        yc = y - mu
        var = jnp.mean(yc * yc, axis=-1, keepdims=True)
        hc_ref[pl.ds(r0, chunk), :] = _silu((yc * lax.rsqrt(var + EPS)) * lg + lb).astype(BF16)
        return carry

    lax.fori_loop(0, rows // chunk, body, 0, unroll=4)


def _merge_math(hc, att, ys, ga, gb, gc, wc, wa, ws):
    a = jnp.dot(hc, wc, preferred_element_type=F32)
    b = jnp.dot(att.astype(BF16), wa, preferred_element_type=F32)
    c = jnp.dot(ys, ws, preferred_element_type=F32)
    return (_sigmoid(ga) * a + _sigmoid(gb) * b + _sigmoid(gc) * c).astype(BF16)


def _merge_kernel(yc_ref, att_ref, ys_ref, ga_ref, gb_ref, gc_ref, cb_ref, lg_ref, lb_ref,
                  wc_ref, wa_ref, ws_ref, o_ref, hc_ref):
    @pl.when(pl.program_id(1) == 0)
    def _():
        _conv_post(yc_ref, cb_ref, lg_ref, lb_ref, hc_ref)

    o_ref[...] = _merge_math(hc_ref[...], att_ref[...], ys_ref[...], ga_ref[...], gb_ref[...], gc_ref[...],
                             wc_ref[...], wa_ref[...], ws_ref[...])


def _merge_cast_kernel(yc_ref, att_ref, ys_ref, ga_ref, gb_ref, gc_ref, cb_ref, lg_ref, lb_ref,
                       wc_ref, wa_ref, ws_ref, o_ref, wcb_ref, wab_ref, wsb_ref, hc_ref):
    @pl.when(pl.program_id(0) == 0)
    def _():
        _conv_post(yc_ref, cb_ref, lg_ref, lb_ref, hc_ref)

    wcb_ref[...] = wc_ref[...].astype(BF16)
    wab_ref[...] = wa_ref[...].astype(BF16)
    wsb_ref[...] = ws_ref[...].astype(BF16)
    o_ref[...] = _merge_math(hc_ref[...], att_ref[...], ys_ref[...], ga_ref[...], gb_ref[...], gc_ref[...],
                             wcb_ref[...], wab_ref[...], wsb_ref[...])


def _merge_cast(yconv, att, ys, z, conv_b, ln_g, ln_b, w_conv, w_attn, w_ssm, *, layer, gate_col, bn):
    m, d_conv = yconv.shape
    d_attn = att.shape[1]
    d_ssm = ys.shape[1]
    d = w_conv.shape[2]
    assert d % bn == 0 and gate_col % bn == 0
    g0 = gate_col // bn
    nj = d // bn
    vec = pl.BlockSpec((1, d_conv), lambda j: (0, 0))
    wspec = lambda k: pl.BlockSpec((None, k, bn), lambda j: (layer, 0, j))
    ospec = lambda k: pl.BlockSpec((k, bn), lambda j: (0, j))
    return pl.pallas_call(
        _merge_cast_kernel,
        grid=(nj,),
        in_specs=[
            pl.BlockSpec((m, d_conv), lambda j: (0, 0)),
            pl.BlockSpec((m, d_attn), lambda j: (0, 0)),
            pl.BlockSpec((m, d_ssm), lambda j: (0, 0)),
            pl.BlockSpec((m, bn), lambda j: (0, g0 + j)),
            pl.BlockSpec((m, bn), lambda j: (0, g0 + nj + j)),
            pl.BlockSpec((m, bn), lambda j: (0, g0 + 2 * nj + j)),
            vec, vec, vec,
            wspec(d_conv), wspec(d_attn), wspec(d_ssm),
        ],
        out_specs=[pl.BlockSpec((m, bn), lambda j: (0, j)), ospec(d_conv), ospec(d_attn), ospec(d_ssm)],
        out_shape=[jax.ShapeDtypeStruct((m, d), BF16), jax.ShapeDtypeStruct((d_conv, d), BF16),
                   jax.ShapeDtypeStruct((d_attn, d), BF16), jax.ShapeDtypeStruct((d_ssm, d), BF16)],
        scratch_shapes=[pltpu.VMEM((m, d_conv), BF16)],
        compiler_params=_cparams(("arbitrary",)),
        name="merge_cast",
    )(yconv, att, ys, z, z, z, conv_b, ln_g, ln_b, w_conv, w_attn, w_ssm)


def _merge(yconv, att, ys, z, conv_b, ln_g, ln_b, w_conv, w_attn, w_ssm, *, gate_col, bm, bn):
    m, d_conv = yconv.shape
    d_attn = att.shape[1]
    d_ssm = ys.shape[1]
    d = w_conv.shape[1]
    assert m % bm == 0 and d % bn == 0 and gate_col % bn == 0
    g0 = gate_col // bn
    nj = d // bn
    vec = pl.BlockSpec((1, d_conv), lambda i, j: (0, 0))
    return pl.pallas_call(
        _merge_kernel,
        grid=(m // bm, nj),
        in_specs=[
            pl.BlockSpec((bm, d_conv), lambda i, j: (i, 0)),
            pl.BlockSpec((bm, d_attn), lambda i, j: (i, 0)),
            pl.BlockSpec((bm, d_ssm), lambda i, j: (i, 0)),
            pl.BlockSpec((bm, bn), lambda i, j: (i, g0 + j)),
            pl.BlockSpec((bm, bn), lambda i, j: (i, g0 + nj + j)),
            pl.BlockSpec((bm, bn), lambda i, j: (i, g0 + 2 * nj + j)),
            vec, vec, vec,
            pl.BlockSpec((d_conv, bn), lambda i, j: (0, j)),
            pl.BlockSpec((d_attn, bn), lambda i, j: (0, j)),
            pl.BlockSpec((d_ssm, bn), lambda i, j: (0, j)),
        ],
        out_specs=pl.BlockSpec((bm, bn), lambda i, j: (i, j)),
        out_shape=jax.ShapeDtypeStruct((m, d), BF16),
        scratch_shapes=[pltpu.VMEM((bm, d_conv), BF16)],
        compiler_params=_cparams(("parallel", "arbitrary")),
        name="merge",
    )(yconv, att, ys, z, z, z, conv_b, ln_g, ln_b, w_conv, w_attn, w_ssm)


def _resproj_kernel(a_ref, w_ref, x_ref, o_ref):
    o_ref[...] = x_ref[...] + jnp.dot(a_ref[...], w_ref[...], preferred_element_type=F32)


def _resproj(a, w, x, *, bm, bn):
    m, k = a.shape
    n = w.shape[1]
    assert m % bm == 0 and n % bn == 0
    return pl.pallas_call(
        _resproj_kernel,
        grid=(m // bm, n // bn),
        in_specs=[
            pl.BlockSpec((bm, k), lambda i, j: (i, 0)),
            pl.BlockSpec((k, bn), lambda i, j: (0, j)),
            pl.BlockSpec((bm, bn), lambda i, j: (i, j)),
        ],
        out_specs=pl.BlockSpec((bm, bn), lambda i, j: (i, j)),
        out_shape=jax.ShapeDtypeStruct((m, n), F32),
        compiler_params=_cparams(("parallel", "arbitrary")),
        name="resproj",
    )(a, w, x)


def _resproj_cast_kernel(a_ref, w_ref, x_ref, o_ref, wb_ref):
    wb_ref[...] = w_ref[...].astype(BF16)
    o_ref[...] = x_ref[...] + jnp.dot(a_ref[...], wb_ref[...], preferred_element_type=F32)


def _resproj_cast(a, w, x, *, layer, bn):
    m, k = a.shape
    n = w.shape[2]
    assert n % bn == 0
    return pl.pallas_call(
        _resproj_cast_kernel,
        grid=(n // bn,),
        in_specs=[
            pl.BlockSpec((m, k), lambda j: (0, 0)),
            pl.BlockSpec((None, k, bn), lambda j: (layer, 0, j)),
            pl.BlockSpec((m, bn), lambda j: (0, j)),
        ],
        out_specs=[pl.BlockSpec((m, bn), lambda j: (0, j)), pl.BlockSpec((k, bn), lambda j: (0, j))],
        out_shape=[jax.ShapeDtypeStruct((m, n), F32), jax.ShapeDtypeStruct((k, n), BF16)],
        compiler_params=_cparams(("arbitrary",)),
        name="resproj_cast",
    )(a, w, x)


def _final_norm_kernel(x_ref, g_ref, o_ref):
    _rms_rows(x_ref, g_ref, o_ref)


def _final_norm(x, g, *, bm):
    m, d = x.shape
    assert m % bm == 0
    return pl.pallas_call(
        _final_norm_kernel,
        grid=(m // bm,),
        in_specs=[pl.BlockSpec((bm, d), lambda i: (i, 0)), pl.BlockSpec((1, d), lambda i: (0, 0))],
        out_specs=pl.BlockSpec((bm, d), lambda i: (i, 0)),
        out_shape=jax.ShapeDtypeStruct((m, d), F32),
        compiler_params=_cparams(("parallel",)),
        name="final_norm",
    )(x, g)


def kernel(x_prompt, x_sample, cache_k, cache_v, page_table, state_conv, state_ssm_re, state_ssm_im, meta_tokens, norm_ffn1, ffn1_gate, ffn1_up, ffn1_down, norm_mix, w_in, conv_w, conv_b, conv_ln_g, conv_ln_b, conv_proj, lambda_q1, lambda_k1, lambda_q2, lambda_k2, attn_subln_g, attn_proj, ssm_a_re, ssm_a_im, ssm_log_dt, ssm_b_re, ssm_b_im, ssm_c_re, ssm_c_im, ssm_d, ssm_glu, ssm_proj, w_out, norm_ffn2, ffn2_gate, ffn2_up, ffn2_down, norm_final):
    bp, seq, d = x_prompt.shape
    db = x_sample.shape[0]
    depth = w_in.shape[0]
    d_conv = conv_w.shape[2]
    hd = lambda_q1.shape[1]
    hw = 2 * hd
    n_heads = cache_k.shape[3]
    d_attn = n_heads * hw
    n_groups, n_state_g = ssm_a_re.shape[1], ssm_a_re.shape[2]
    d_ssm = ssm_d.shape[1]
    n_state = n_groups * n_state_g
    t = seq + N_META
    mp = bp * t
    q_off = 2 * d_conv
    k_off = q_off + d_attn
    v_off = k_off + d_attn
    u_off = v_off + d_attn
    gate_off = u_off + d_ssm

    bm_p = t // 3
    bm_ffn = t // 2
    row2 = lambda v: v.reshape(1, -1)

    xp = jnp.concatenate([jnp.broadcast_to(meta_tokens[None], (bp, N_META, d)), x_prompt], axis=1).reshape(mp, d)
    xs = x_sample.reshape(db, d)
    zero_conv = jnp.zeros((bp, CONV_PAD, d_conv), F32)

    outs_p = {k: [] for k in ("k", "v", "conv", "sre", "sim")}
    outs_s = {k: [] for k in ("k", "v", "conv", "sre", "sim")}
    for l in range(depth):
        lam_init = 0.8 - 0.6 * math.exp(-0.3 * l)
        lq1, lk1, lq2, lk2 = row2(lambda_q1[l]), row2(lambda_k1[l]), row2(lambda_q2[l]), row2(lambda_k2[l])
        sub_g = row2(attn_subln_g[l])
        cb, lng, lnb = row2(conv_b[l]), row2(conv_ln_g[l]), row2(conv_ln_b[l])

        abr, abi, bbr, bbi = _ssm_prep(ssm_a_re[l], ssm_a_im[l], ssm_log_dt[l], ssm_b_re[l], ssm_b_im[l])
        bre = _block_diag_in(bbr, n_groups, n_state_g).astype(BF16)
        bim = _block_diag_in(bbi, n_groups, n_state_g).astype(BF16)
        cre = _block_diag_out(ssm_c_re[l], n_groups).astype(BF16)
        cim = _block_diag_out(ssm_c_im[l], n_groups).astype(BF16)
        dvec = row2(ssm_d[l])

        xs, wg1, wu1, wd1 = _ffn_cast(xs, row2(norm_ffn1[l]), ffn1_gate, ffn1_up, ffn1_down, layer=l, bf=FFN_CAST_BF)
        xp = _ffn(xp, row2(norm_ffn1[l]), wg1, wu1, wd1, bm=bm_ffn, bf=FFN_BF)

        zs, win = _inproj_cast(xs, row2(norm_mix[l]), w_in, layer=l, bn=512)
        zp = _inproj(xp, row2(norm_mix[l]), win, bm=bm_p, bn=1024)
        zp3 = zp.reshape(bp, t, -1)

        yconv_s, conv_s = _conv_step(zs, state_conv[l], conv_w[l], d_conv=d_conv)
        q_s = zs[:, q_off:k_off].reshape(db, n_heads, hw)
        k_s = zs[:, k_off:v_off].reshape(db, n_heads, hw)
        v_s = zs[:, v_off:u_off].reshape(db, n_heads, hw)
        att_s = _decode_attn(page_table, q_s, cache_k, cache_v, k_s, v_s, lq1, lk1, lq2, lk2, sub_g,
                             layer=l, lam_init=lam_init, pages=DECODE_PAGES)
        ys_s, sre_s, sim_s, wglu = _ssm_step(zs, state_ssm_re[l].reshape(db, n_state),
                                             state_ssm_im[l].reshape(db, n_state), abr, abi, bre, bim, cre, cim, dvec,
                                             ssm_glu, layer=l, d_ssm=d_ssm, u_col=u_off // d_ssm)

        yconv_p, conv_p = _conv_seq(zp3, zero_conv, conv_w[l], d_conv=d_conv, cb=256, rows=48)
        att_p, k_p, v_p = _flash(zp3, lq1, lk1, lq2, lk2, sub_g, n_heads=n_heads, hd=hd, q_col=q_off // hw,
                                 k_col=k_off // hw, v_col=v_off // hw, tq=t // 3, tk=t // 3, lam_init=lam_init)
        outs_p["k"].append(k_p.reshape(bp, t, n_heads, hw))
        outs_p["v"].append(v_p.reshape(bp, t, n_heads, hw))
        ys_p, sre_p, sim_p = _ssm_seq(zp3, abr, abi, bre, bim, cre, cim, dvec, wglu,
                                      d_ssm=d_ssm, u_col=u_off // d_ssm, lc_rows=t // 3)

        merged_s, wconv, wattn, wssm = _merge_cast(yconv_s, att_s.reshape(db, d_attn), ys_s, zs, cb, lng, lnb,
                                                   conv_proj, attn_proj, ssm_proj, layer=l, gate_col=gate_off, bn=512)
        merged_p = _merge(yconv_p.reshape(mp, d_conv), att_p.reshape(mp, d_attn), ys_p.reshape(mp, d_ssm), zp,
                          cb, lng, lnb, wconv, wattn, wssm, gate_col=gate_off, bm=bm_p, bn=512)
        xs, wout = _resproj_cast(merged_s, w_out, xs, layer=l, bn=512)
        xp = _resproj(merged_p, wout, xp, bm=bm_p, bn=1024)

        xs, wg2, wu2, wd2 = _ffn_cast(xs, row2(norm_ffn2[l]), ffn2_gate, ffn2_up, ffn2_down, layer=l, bf=FFN_CAST_BF)
        xp = _ffn(xp, row2(norm_ffn2[l]), wg2, wu2, wd2, bm=bm_ffn, bf=FFN_BF)

        outs_p["conv"].append(conv_p)
        outs_p["sre"].append(sre_p.reshape(bp, n_groups, n_state_g))
        outs_p["sim"].append(sim_p.reshape(bp, n_groups, n_state_g))
        outs_s["k"].append(k_s.reshape(db, 1, n_heads, hw))
        outs_s["v"].append(v_s.reshape(db, 1, n_heads, hw))
        outs_s["conv"].append(conv_s)
        outs_s["sre"].append(sre_s.reshape(db, n_groups, n_state_g))
        outs_s["sim"].append(sim_s.reshape(db, n_groups, n_state_g))

    y_prompt = _final_norm(xp, row2(norm_final), bm=bm_p).reshape(bp, t, d)[:, N_META:]
    y_sample = _final_norm(xs, row2(norm_final), bm=db).reshape(db, 1, d)
    st = jnp.stack
    return (y_prompt, y_sample, st(outs_p["k"]), st(outs_p["v"]), st(outs_p["conv"]), st(outs_p["sre"]),
            st(outs_p["sim"]), st(outs_s["k"]), st(outs_s["v"]), st(outs_s["conv"]), st(outs_s["sre"]),
            st(outs_s["sim"]))
```
